```python
import math
import jax, jax.numpy as jnp
from jax import lax
import numpy as np

D_MODEL = 2048
BATCH = 2
SEQ = 4096
DEPTH = 4
DEC_BATCH = 8
DEC_SEQ = 1
PAST_LEN = 16384
PAGE_SIZE = 128

N_EVEN = (DEPTH + 1) // 2
N_ODD = DEPTH // 2
HALF = D_MODEL // 2
DIFF_HEADS = 8
DIFF_DK = HALF // (2 * DIFF_HEADS)
DIFF_DV = 2 * DIFF_DK
DIFF_ROT = DIFF_DK // 4
ROPE_THETA = 500000.0
CONV_CH = HALF
CONV_WIDTH = 31
FOX_HEADS = 16
FOX_DH = D_MODEL // FOX_HEADS
FOX_FORGET_INIT = 3.0
MEM_LEN = 256
X_HEADS = 4
X_DH = 128
X_WIDTH = X_HEADS * X_DH
PEER_HEADS = 8
PEER_NKEYS = 128
PEER_EXPERTS = PEER_NKEYS * PEER_NKEYS
PEER_QDIM = 256
PEER_HALF = PEER_QDIM // 2
PEER_TOPK = 16
PEER_BLOCK = 256
Q_BLOCK = 128
EPS = 1e-6

kernel_name = "hybrid_diffattn_conformer_fox_peer_decode_step"

F32 = jnp.float32


def rmsnorm(x, g):
    xf = x.astype(F32)
    y = xf * lax.rsqrt(jnp.mean(xf * xf, axis=-1, keepdims=True) + EPS)
    return (y * g.astype(F32)).astype(x.dtype)


def layernorm(x, g, b):
    xf = x.astype(F32)
    mu = jnp.mean(xf, axis=-1, keepdims=True)
    var = jnp.mean(jnp.square(xf - mu), axis=-1, keepdims=True)
    y = (xf - mu) * lax.rsqrt(var + EPS) * g.astype(F32) + b.astype(F32)
    return y.astype(x.dtype)


def rope_partial(x, pos):
    half = DIFF_ROT // 2
    inv = jnp.power(ROPE_THETA, -jnp.arange(half, dtype=F32) * 2.0 / DIFF_ROT)
    ang = pos.astype(F32)[:, None] * inv[None, :]
    shape = (ang.shape[0],) + (1,) * (x.ndim - 3) + (half,)
    cos = jnp.cos(ang).reshape(shape)
    sin = jnp.sin(ang).reshape(shape)
    xf = x.astype(F32)
    x1 = xf[..., :half]
    x2 = xf[..., half:DIFF_ROT]
    out = jnp.concatenate([x1 * cos - x2 * sin, x2 * cos + x1 * sin, xf[..., DIFF_ROT:]], axis=-1)
    return out.astype(x.dtype)


def blocked_queries(fn, q_parts, seq_len):
    nb = seq_len // Q_BLOCK

    def split(a):
        return jnp.moveaxis(a.reshape((a.shape[0], nb, Q_BLOCK) + a.shape[2:]), 1, 0)

    pos = jnp.arange(seq_len, dtype=jnp.int32).reshape(nb, Q_BLOCK)
    out = lax.map(lambda a: fn(*a), tuple(split(a) for a in q_parts) + (pos,))
    out = jnp.moveaxis(out, 0, 1)
    return out.reshape((out.shape[0], seq_len) + out.shape[3:])


def diff_lambda_value(lam_params, lam_init):
    lp = lam_params.astype(F32)
    return jnp.exp(jnp.sum(lp[0] * lp[1])) - jnp.exp(jnp.sum(lp[2] * lp[3])) + lam_init


def even_project(h, w_in, q_gain, k_gain, pos):
    b, s, _ = h.shape
    z = h @ w_in
    q, k, v, ga, gb = jnp.split(z, 5, axis=-1)
    q = rope_partial(rmsnorm(q.reshape(b, s, DIFF_HEADS, 2, DIFF_DK), q_gain), pos)
    k = rope_partial(rmsnorm(k.reshape(b, s, DIFF_HEADS, 2, DIFF_DK), k_gain), pos)
    v = v.reshape(b, s, DIFF_HEADS, DIFF_DV)
    u = ga * jax.nn.sigmoid(gb)
    return q, k, v, u


def diff_core(q, k, v, q_pos, k_pos, lam):
    s = jnp.einsum('bqhcd,bkhcd->bhcqk', q, k).astype(F32) * (DIFF_DK ** -0.5)
    mask = k_pos[None, :] <= q_pos[:, None]
    p = jax.nn.softmax(jnp.where(mask, s, -jnp.inf), axis=-1)
    a = p[:, :, 0] - lam * p[:, :, 1]
    return jnp.einsum('bhqk,bkhd->bqhd', a.astype(v.dtype), v)


def conv_branch(u_pad, w, b, ln_g, ln_b):
    y = lax.conv_general_dilated(
        u_pad, w[:, None, :].astype(u_pad.dtype), window_strides=(1,), padding='VALID',
        dimension_numbers=('NWC', 'WIO', 'NWC'), feature_group_count=u_pad.shape[-1])
    y = layernorm(y + b, ln_g, ln_b)
    return jax.nn.silu(y)


def even_merge(attn, conv, out_gain, lam_init, w_out):
    b, s = attn.shape[:2]
    o = (rmsnorm(attn, out_gain) * (1.0 - lam_init)).reshape(b, s, HALF)
    return jnp.concatenate([o, conv.astype(o.dtype)], axis=-1) @ w_out


def odd_project(h, w_in, f_bias, q_gain, k_gain):
    b, s, _ = h.shape
    z = h @ w_in
    q = rmsnorm(z[..., :D_MODEL].reshape(b, s, FOX_HEADS, FOX_DH), q_gain)
    k = rmsnorm(z[..., D_MODEL:2 * D_MODEL].reshape(b, s, FOX_HEADS, FOX_DH), k_gain)
    v = z[..., 2 * D_MODEL:3 * D_MODEL].reshape(b, s, FOX_HEADS, FOX_DH)
    logf = jax.nn.log_sigmoid(z[..., 3 * D_MODEL:].astype(F32) + f_bias.astype(F32))
    return q, k, v, logf


def fox_core(q, k, v, cq, ck, q_pos, k_pos):
    s = jnp.einsum('bqhd,bkhd->bhqk', q, k).astype(F32) * (FOX_DH ** -0.5)
    s = s + (jnp.swapaxes(cq, 1, 2)[:, :, :, None] - jnp.swapaxes(ck, 1, 2)[:, :, None, :])
    mask = k_pos[None, :] <= q_pos[:, None]
    p = jax.nn.softmax(jnp.where(mask, s, -jnp.inf), axis=-1)
    return jnp.einsum('bhqk,bkhd->bqhd', p.astype(v.dtype), v)


def mem_kv(mem, mem_g, wk, wv, k_gain):
    b, m, _ = mem.shape
    mn = rmsnorm(mem, mem_g)
    k = rmsnorm((mn @ wk).reshape(b, m, X_HEADS, X_DH), k_gain)
    v = (mn @ wv).reshape(b, m, X_HEADS, X_DH)
    return k, v


def cross_attend(h, k, v, wq, q_gain, wo):
    b, s, _ = h.shape
    q = rmsnorm((h @ wq).reshape(b, s, X_HEADS, X_DH), q_gain)
    sc = jnp.einsum('bqhd,bkhd->bhqk', q, k).astype(F32) * (X_DH ** -0.5)
    p = jax.nn.softmax(sc, axis=-1)
    o = jnp.einsum('bhqk,bkhd->bqhd', p.astype(v.dtype), v).reshape(b, s, X_WIDTH)
    return o @ wo


def peer(h, wq, sub_keys, u_tab, v_tab):
    b, s, d = h.shape
    t = b * s
    x = h.reshape(t, d)
    q = (x @ wq).reshape(t, PEER_HEADS, 2, PEER_HALF)
    sc = jnp.einsum('thcd,cnd->thcn', q, sub_keys).astype(F32)
    v1, i1 = lax.top_k(sc[:, :, 0], PEER_TOPK)
    v2, i2 = lax.top_k(sc[:, :, 1], PEER_TOPK)
    cand = (v1[..., :, None] + v2[..., None, :]).reshape(t, PEER_HEADS, PEER_TOPK * PEER_TOPK)
    cv, ci = lax.top_k(cand, PEER_TOPK)
    e = (jnp.take_along_axis(i1, ci // PEER_TOPK, axis=-1) * PEER_NKEYS
         + jnp.take_along_axis(i2, ci % PEER_TOPK, axis=-1))
    g = jax.nn.softmax(cv, axis=-1)
    blk = min(PEER_BLOCK, t)
    nb = -(-t // blk)
    pad = nb * blk - t
    xb = jnp.pad(x, ((0, pad), (0, 0))).reshape(nb, blk, d)
    eb = jnp.pad(e, ((0, pad), (0, 0), (0, 0))).reshape(nb, blk, PEER_HEADS, PEER_TOPK)
    gb = jnp.pad(g, ((0, pad), (0, 0), (0, 0))).reshape(nb, blk, PEER_HEADS, PEER_TOPK)

    def expert_block(args):
        xs, es, gs = args
        u = jnp.take(u_tab, es, axis=0)
        hh = jnp.einsum('td,thkd->thk', xs, u).astype(F32)
        a = gs * jax.nn.gelu(hh)
        vv = jnp.take(v_tab, es, axis=0)
        return jnp.einsum('thk,thkd->td', a.astype(vv.dtype), vv)

    out = lax.map(expert_block, (xb, eb, gb)).reshape(nb * blk, d)[:t]
    return out.reshape(b, s, d).astype(h.dtype)


def setup_inputs(seed: int = 0) -> dict:
    key = jax.random.key(seed)
    ks = iter(jax.random.split(key, 64))

    def nrm(shape, scale):
        return jax.random.normal(next(ks), shape, F32) * scale

    def gain(shape):
        return 1.0 + nrm(shape, 0.05)

    n_pages = PAST_LEN // PAGE_SIZE
    n_phys = (5 * DEC_BATCH * n_pages + 3) // 4
    page_table = jax.random.permutation(next(ks), n_phys)[: DEC_BATCH * n_pages]
    page_table = page_table.reshape(DEC_BATCH, n_pages).astype(jnp.int32)
    ds = D_MODEL ** -0.5
    return {
        'x_prompt': nrm((BATCH, SEQ, D_MODEL), 1.0),
        'x_sample': nrm((DEC_BATCH, DEC_SEQ, D_MODEL), 1.0),
        'mem_prompt': nrm((BATCH, MEM_LEN, D_MODEL), 1.0),
        'cache_diff_k': nrm((N_EVEN, n_phys, PAGE_SIZE, DIFF_HEADS, 2, DIFF_DK), 1.0),
        'cache_diff_v': nrm((N_EVEN, n_phys, PAGE_SIZE, DIFF_HEADS, DIFF_DV), 1.0),
        'cache_fox_k': nrm((N_ODD, n_phys, PAGE_SIZE, FOX_HEADS, FOX_DH), 1.0),
        'cache_fox_v': nrm((N_ODD, n_phys, PAGE_SIZE, FOX_HEADS, FOX_DH), 1.0),
        'cache_fox_logf': jax.nn.log_sigmoid(FOX_FORGET_INIT + nrm((N_ODD, n_phys, PAGE_SIZE, FOX_HEADS), 1.0)),
        'state_conv': nrm((N_EVEN, DEC_BATCH, CONV_WIDTH - 1, CONV_CH), 0.5),
        'cache_mem_k': nrm((DEPTH, DEC_BATCH, MEM_LEN, X_HEADS, X_DH), 1.0),
        'cache_mem_v': nrm((DEPTH, DEC_BATCH, MEM_LEN, X_HEADS, X_DH), 1.0),
        'page_table': page_table,
        'norm_mix': gain((DEPTH, D_MODEL)),
        'norm_xattn': gain((DEPTH, D_MODEL)),
        'norm_ffn': gain((DEPTH, D_MODEL)),
        'even_w_in': nrm((N_EVEN, D_MODEL, 5 * HALF), ds),
        'even_w_out': nrm((N_EVEN, D_MODEL, D_MODEL), ds),
        'diff_q_gain': gain((N_EVEN, DIFF_DK)),
        'diff_k_gain': gain((N_EVEN, DIFF_DK)),
        'diff_lambda': nrm((N_EVEN, 4, DIFF_DK), 0.1),
        'diff_out_gain': gain((N_EVEN, DIFF_DV)),
        'conv_w': nrm((N_EVEN, CONV_WIDTH, CONV_CH), CONV_WIDTH ** -0.5),
        'conv_b': nrm((N_EVEN, CONV_CH), 0.02),
        'conv_ln_g': gain((N_EVEN, CONV_CH)),
        'conv_ln_b': nrm((N_EVEN, CONV_CH), 0.02),
        'odd_w_in': nrm((N_ODD, D_MODEL, 3 * D_MODEL + FOX_HEADS), ds),
        'fox_forget_bias': FOX_FORGET_INIT + nrm((N_ODD, FOX_HEADS), 0.1),
        'fox_q_gain': gain((N_ODD, FOX_DH)),
        'fox_k_gain': gain((N_ODD, FOX_DH)),
        'odd_w_out': nrm((N_ODD, D_MODEL, D_MODEL), ds),
        'mem_norm': gain((DEPTH, D_MODEL)),
        'xattn_wq': nrm((DEPTH, D_MODEL, X_WIDTH), ds),
        'xattn_wk': nrm((DEPTH, D_MODEL, X_WIDTH), ds),
        'xattn_wv': nrm((DEPTH, D_MODEL, X_WIDTH), ds),
        'xattn_wo': nrm((DEPTH, X_WIDTH, D_MODEL), X_WIDTH ** -0.5),
        'xattn_q_gain': gain((DEPTH, X_DH)),
        'xattn_k_gain': gain((DEPTH, X_DH)),
        'peer_wq': nrm((DEPTH, D_MODEL, PEER_HEADS * PEER_QDIM), ds),
        'peer_keys': nrm((DEPTH, 2, PEER_NKEYS, PEER_HALF), PEER_HALF ** -0.5),
        'peer_u': nrm((DEPTH, PEER_EXPERTS, D_MODEL), ds),
        'peer_v': nrm((DEPTH, PEER_EXPERTS, D_MODEL), (PEER_HEADS * PEER_TOPK) ** -0.5),
    }


def reference(x_prompt, x_sample, mem_prompt, cache_diff_k, cache_diff_v, cache_fox_k, cache_fox_v,
              cache_fox_logf, state_conv, cache_mem_k, cache_mem_v, page_table,
              norm_mix, norm_xattn, norm_ffn, even_w_in, even_w_out, diff_q_gain, diff_k_gain,
              diff_lambda, diff_out_gain, conv_w, conv_b, conv_ln_g, conv_ln_b, odd_w_in,
              fox_forget_bias, fox_q_gain, fox_k_gain, odd_w_out, mem_norm, xattn_wq, xattn_wk,
              xattn_wv, xattn_wo, xattn_q_gain, xattn_k_gain, peer_wq, peer_keys, peer_u, peer_v):
    past = page_table.shape[1] * PAGE_SIZE
    bp, seq = x_prompt.shape[:2]
    bs, dseq = x_sample.shape[:2]
    pos_p = jnp.arange(seq, dtype=jnp.int32)
    pos_s = past + jnp.arange(dseq, dtype=jnp.int32)
    kpos_s = jnp.arange(past + dseq, dtype=jnp.int32)

    def gather_pages(pool):
        g = pool[page_table]
        return g.reshape((bs, past) + pool.shape[2:])

    xp, xs = x_prompt, x_sample
    dk_p, dv_p, dk_s, dv_s = [], [], [], []
    fk_p, fv_p, fl_p, fk_s, fv_s, fl_s = [], [], [], [], [], []
    cv_p, cv_s, mk_p, mv_p = [], [], [], []

    for l in range(DEPTH):
        hp = rmsnorm(xp, norm_mix[l])
        hs = rmsnorm(xs, norm_mix[l])
        if l % 2 == 0:
            i = l // 2
            lam_init = 0.8 - 0.6 * math.exp(-0.3 * l)
            lam = diff_lambda_value(diff_lambda[i], lam_init)
            qp, kp, vp, up = even_project(hp, even_w_in[i], diff_q_gain[i], diff_k_gain[i], pos_p)
            qs, ks_, vs, us = even_project(hs, even_w_in[i], diff_q_gain[i], diff_k_gain[i], pos_s)
            ap = blocked_queries(lambda qb, pb: diff_core(qb, kp, vp, pb, pos_p, lam), (qp,), seq)
            k_all = jnp.concatenate([gather_pages(cache_diff_k[i]), ks_], axis=1)
            v_all = jnp.concatenate([gather_pages(cache_diff_v[i]), vs], axis=1)
            as_ = diff_core(qs, k_all, v_all, pos_s, kpos_s, lam)
            up_pad = jnp.concatenate([jnp.zeros((bp, CONV_WIDTH - 1, CONV_CH), up.dtype), up], axis=1)
            us_pad = jnp.concatenate([state_conv[i], us], axis=1)
            cp = conv_branch(up_pad, conv_w[i], conv_b[i], conv_ln_g[i], conv_ln_b[i])
            cs = conv_branch(us_pad, conv_w[i], conv_b[i], conv_ln_g[i], conv_ln_b[i])
            xp = xp + even_merge(ap, cp, diff_out_gain[i], lam_init, even_w_out[i])
            xs = xs + even_merge(as_, cs, diff_out_gain[i], lam_init, even_w_out[i])
            dk_p.append(kp); dv_p.append(vp); dk_s.append(ks_); dv_s.append(vs)
            cv_p.append(up_pad[:, -(CONV_WIDTH - 1):]); cv_s.append(us_pad[:, -(CONV_WIDTH - 1):])
        else:
            i = l // 2
            qp, kp, vp, lfp = odd_project(hp, odd_w_in[i], fox_forget_bias[i], fox_q_gain[i], fox_k_gain[i])
            qs, ks_, vs, lfs = odd_project(hs, odd_w_in[i], fox_forget_bias[i], fox_q_gain[i], fox_k_gain[i])
            cum_p = jnp.cumsum(lfp, axis=1)
            ap = blocked_queries(lambda qb, cb, pb: fox_core(qb, kp, vp, cb, cum_p, pb, pos_p), (qp, cum_p), seq)
            k_all = jnp.concatenate([gather_pages(cache_fox_k[i]), ks_], axis=1)
            v_all = jnp.concatenate([gather_pages(cache_fox_v[i]), vs], axis=1)
            lf_all = jnp.concatenate([gather_pages(cache_fox_logf[i]).astype(F32), lfs], axis=1)
            cum_s = jnp.cumsum(lf_all, axis=1)
            as_ = fox_core(qs, k_all, v_all, cum_s[:, past:], cum_s, pos_s, kpos_s)
            xp = xp + ap.reshape(bp, seq, D_MODEL) @ odd_w_out[i]
            xs = xs + as_.reshape(bs, dseq, D_MODEL) @ odd_w_out[i]
            fk_p.append(kp); fv_p.append(vp); fl_p.append(lfp)
            fk_s.append(ks_); fv_s.append(vs); fl_s.append(lfs)
        mkp, mvp = mem_kv(mem_prompt, mem_norm[l], xattn_wk[l], xattn_wv[l], xattn_k_gain[l])
        mk_p.append(mkp); mv_p.append(mvp)
        xp = xp + cross_attend(rmsnorm(xp, norm_xattn[l]), mkp, mvp, xattn_wq[l], xattn_q_gain[l], xattn_wo[l])
        xs = xs + cross_attend(rmsnorm(xs, norm_xattn[l]), cache_mem_k[l], cache_mem_v[l],
                               xattn_wq[l], xattn_q_gain[l], xattn_wo[l])
        xp = xp + peer(rmsnorm(xp, norm_ffn[l]), peer_wq[l], peer_keys[l], peer_u[l], peer_v[l])
        xs = xs + peer(rmsnorm(xs, norm_ffn[l]), peer_wq[l], peer_keys[l], peer_u[l], peer_v[l])

    return (xp, xs,
            jnp.stack(dk_p), jnp.stack(dv_p), jnp.stack(fk_p), jnp.stack(fv_p), jnp.stack(fl_p),
            jnp.stack(cv_p), jnp.stack(mk_p), jnp.stack(mv_p),
            jnp.stack(dk_s), jnp.stack(dv_s), jnp.stack(fk_s), jnp.stack(fv_s), jnp.stack(fl_s),
            jnp.stack(cv_s))
```

```python
import functools
import math

import jax
import jax.numpy as jnp
from jax import lax
from jax.experimental import pallas as pl
from jax.experimental.pallas import tpu as pltpu

F32 = jnp.float32
BF16 = jnp.bfloat16
I32 = jnp.int32

EPS = 1e-6
NEG = -1e30
ROPE_THETA = 500000.0
LANES = 128
V7X_VMEM_BYTES = 64 * 1024 * 1024
VMEM_LIMIT = V7X_VMEM_BYTES - 8 * 1024 * 1024

DIFF_HEADS = 8
DIFF_DK = 64
DIFF_ROT = 16
FOX_HEADS = 16
X_HEADS = 4
PEER_HEADS = 8
PEER_NKEYS = 128
PEER_TOPK = 16
CONV_WIDTH = 31
CONV_HALO = 32
PAGE = 128


def _params(*sem):
    return pltpu.CompilerParams(dimension_semantics=sem, vmem_limit_bytes=VMEM_LIMIT)


def _rms_matmul_kernel(*refs, use_norm, has_res):
    if has_res:
        x_ref, g_ref, w_ref, r_ref, o_ref, xn_ref = refs
    else:
        x_ref, g_ref, w_ref, o_ref, xn_ref = refs

    @pl.when(pl.program_id(1) == 0)
    def _():
        x = x_ref[...]
        if use_norm:
            x = x * lax.rsqrt(jnp.mean(x * x, axis=-1, keepdims=True) + EPS) * g_ref[...]
        xn_ref[...] = x.astype(BF16)

    acc = jnp.dot(xn_ref[...], w_ref[...], preferred_element_type=F32)
    if has_res:
        acc = r_ref[...] + acc
    o_ref[...] = acc


def rms_matmul(x, gain, w, res=None, tm=1024, tn=512):
    m, k = x.shape
    n = w.shape[1]
    tm = min(tm, m)
    tn = min(tn, n)
    assert m % tm == 0 and n % tn == 0, (m, n, tm, tn)
    use_norm = gain is not None
    g = (gain if use_norm else jnp.ones((k,), F32)).reshape(1, k).astype(F32)
    in_specs = [pl.BlockSpec((tm, k), lambda i, j: (i, 0)),
                pl.BlockSpec((1, k), lambda i, j: (0, 0)),
                pl.BlockSpec((k, tn), lambda i, j: (0, j))]
    args = [x, g, w]
    if res is not None:
        in_specs.append(pl.BlockSpec((tm, tn), lambda i, j: (i, j)))
        args.append(res)
    return pl.pallas_call(
        functools.partial(_rms_matmul_kernel, use_norm=use_norm, has_res=res is not None),
        grid=(m // tm, n // tn),
        in_specs=in_specs,
        out_specs=pl.BlockSpec((tm, tn), lambda i, j: (i, j)),
        out_shape=jax.ShapeDtypeStruct((m, n), F32),
        scratch_shapes=[pltpu.VMEM((tm, k), BF16)],
        compiler_params=_params("parallel", "arbitrary"),
        name="rms_matmul",
    )(*args)


def _head_norm_kernel(*refs, gw, rope, out_scale):
    if rope:
        a_ref, g_ref, cos_ref, sin_ref, o_ref = refs
    else:
        a_ref, g_ref, o_ref = refs
    x = a_ref[...]
    xx = x * x
    lane = lax.broadcasted_iota(I32, x.shape, 1)
    if gw == LANES:
        ms = jnp.mean(xx, axis=-1, keepdims=True)
    else:
        lo = lane < gw
        s_lo = jnp.sum(jnp.where(lo, xx, 0.0), axis=-1, keepdims=True)
        s_hi = jnp.sum(jnp.where(lo, 0.0, xx), axis=-1, keepdims=True)
        ms = jnp.where(lo, s_lo, s_hi) * (1.0 / gw)
    y = x * lax.rsqrt(ms + EPS) * g_ref[...]
    if rope:
        half = DIFF_ROT // 2
        first = (lane & (gw - 1)) < half
        partner = jnp.where(first, pltpu.roll(y, LANES - half, 1), pltpu.roll(y, half, 1))
        y = y * cos_ref[...] + partner * sin_ref[...]
    if out_scale != 1.0:
        y = y * out_scale
    o_ref[...] = y


def head_norm(a, col0, ncols, gain, gw, rope_tabs=None, out_scale=1.0, tm=1024):
    m = a.shape[0]
    tm = min(tm, m)
    if rope_tabs is not None:
        tm = min(tm, rope_tabs[0].shape[0])
    assert m % tm == 0
    g = jnp.tile(gain.astype(F32), LANES // gw).reshape(1, LANES)
    in_specs = [pl.BlockSpec((tm, LANES), lambda i, j: (i, col0 + j)),
                pl.BlockSpec((1, LANES), lambda i, j: (0, 0))]
    args = [a, g]
    if rope_tabs is not None:
        cos_t, sin_t = rope_tabs
        nrep = cos_t.shape[0] // tm
        in_specs += [pl.BlockSpec((tm, LANES), lambda i, j: (i % nrep, 0)),
                     pl.BlockSpec((tm, LANES), lambda i, j: (i % nrep, 0))]
        args += [cos_t, sin_t]
    return pl.pallas_call(
        functools.partial(_head_norm_kernel, gw=gw, rope=rope_tabs is not None, out_scale=out_scale),
        grid=(m // tm, ncols),
        in_specs=in_specs,
        out_specs=pl.BlockSpec((tm, LANES), lambda i, j: (i, j)),
        out_shape=jax.ShapeDtypeStruct((m, ncols * LANES), F32),
        compiler_params=_params("parallel", "parallel"),
        name="head_norm",
    )(*args)


def rope_tables(pos, rows):
    half = DIFF_ROT // 2
    inv = jnp.power(ROPE_THETA, -jnp.arange(half, dtype=F32) * 2.0 / DIFF_ROT)
    ang = pos.astype(F32)[:, None] * inv[None, :]
    cos, sin = jnp.cos(ang), jnp.sin(ang)
    ones = jnp.ones((pos.shape[0], DIFF_DK - DIFF_ROT), F32)
    cos64 = jnp.concatenate([cos, cos, ones], axis=1)
    sin64 = jnp.concatenate([-sin, sin, 0.0 * ones], axis=1)
    cos_t = jnp.tile(cos64, (1, LANES // DIFF_DK))
    sin_t = jnp.tile(sin64, (1, LANES // DIFF_DK))
    if cos_t.shape[0] != rows:
        cos_t = jnp.broadcast_to(cos_t[:1], (rows, LANES))
        sin_t = jnp.broadcast_to(sin_t[:1], (rows, LANES))
    return cos_t, sin_t


def _glu_kernel(a_ref, b_ref, o_ref):
    o_ref[...] = a_ref[...] * jax.nn.sigmoid(b_ref[...])


def glu(z, cola, colb, width, tm=1024, tn=512):
    m = z.shape[0]
    tm = min(tm, m)
    nb = width // tn
    return pl.pallas_call(
        _glu_kernel,
        grid=(m // tm, nb),
        in_specs=[pl.BlockSpec((tm, tn), lambda i, j: (i, cola // tn + j)),
                  pl.BlockSpec((tm, tn), lambda i, j: (i, colb // tn + j))],
        out_specs=pl.BlockSpec((tm, tn), lambda i, j: (i, j)),
        out_shape=jax.ShapeDtypeStruct((m, width), F32),
        compiler_params=_params("parallel", "parallel"),
        name="glu",
    )(z, z)


def _conv_kernel(cur_ref, prev_ref, init_ref, w_ref, b_ref, g_ref, beta_ref, o_ref, buf_ref, *, tt, rc):
    i = pl.program_id(1)

    @pl.when(i == 0)
    def _():
        buf_ref[0:CONV_HALO, :] = init_ref[0]

    if tt >= CONV_HALO:
        @pl.when(i > 0)
        def _():
            buf_ref[0:CONV_HALO, :] = prev_ref[0, tt - CONV_HALO:tt, :]

    buf_ref[CONV_HALO:CONV_HALO + tt, :] = cur_ref[0]
    off = CONV_HALO - (CONV_WIDTH - 1)
    for r0 in range(0, tt, rc):
        acc = jnp.zeros((rc, cur_ref.shape[2]), F32)
        for j in range(CONV_WIDTH):
            acc = acc + w_ref[j:j + 1, :] * buf_ref[r0 + off + j:r0 + off + j + rc, :]
        y = acc + b_ref[...]
        mu = jnp.mean(y, axis=-1, keepdims=True)
        yc = y - mu
        var = jnp.mean(yc * yc, axis=-1, keepdims=True)
        y = yc * lax.rsqrt(var + EPS) * g_ref[...] + beta_ref[...]
        o_ref[0, r0:r0 + rc, :] = y * jax.nn.sigmoid(y)


def conv_branch(u, init, w, b, ln_g, ln_b, tt=128):
    bsz, length, ch = u.shape
    tt = min(tt, length)
    assert length % tt == 0 and (tt >= CONV_HALO or length == tt)
    rc = min(16, tt)
    wpad = jnp.concatenate([w, jnp.zeros((CONV_HALO - CONV_WIDTH, ch), F32)], axis=0)
    row = lambda v: v.reshape(1, ch).astype(F32)
    return pl.pallas_call(
        functools.partial(_conv_kernel, tt=tt, rc=rc),
        grid=(bsz, length // tt),
        in_specs=[pl.BlockSpec((1, tt, ch), lambda bi, i: (bi, i, 0)),
                  pl.BlockSpec((1, tt, ch), lambda bi, i: (bi, jnp.maximum(i - 1, 0), 0)),
                  pl.BlockSpec((1, CONV_HALO, ch), lambda bi, i: (bi, 0, 0)),
                  pl.BlockSpec((CONV_HALO, ch), lambda bi, i: (0, 0)),
                  pl.BlockSpec((1, ch), lambda bi, i: (0, 0)),
                  pl.BlockSpec((1, ch), lambda bi, i: (0, 0)),
                  pl.BlockSpec((1, ch), lambda bi, i: (0, 0))],
        out_specs=pl.BlockSpec((1, tt, ch), lambda bi, i: (bi, i, 0)),
        out_shape=jax.ShapeDtypeStruct((bsz, length, ch), F32),
        scratch_shapes=[pltpu.VMEM((CONV_HALO + tt, ch), F32)],
        compiler_params=_params("parallel", "arbitrary"),
        name="conv_branch",
    )(u, u, init, wpad, row(b), row(ln_g), row(ln_b))


def _flash_kernel(*refs, mode, causal, scale, tq, tk, lam_init):
    if mode == "fox":
        q_ref, k_ref, v_ref, cq_ref, ck_ref, o_ref, qs_ref, m_ref, l_ref, acc_ref, cqs_ref = refs
    elif mode == "diff":
        q_ref, k_ref, v_ref, lam_ref, o_ref, qs_ref, m_ref, l_ref, acc_ref = refs
    else:
        q_ref, k_ref, v_ref, o_ref, qs_ref, m_ref, l_ref, acc_ref = refs
    h = pl.program_id(1)
    qi = pl.program_id(2)
    ki = pl.program_id(3)
    nk = pl.num_programs(3)
    rows = 2 * tq if mode == "diff" else tq

    @pl.when(ki == 0)
    def _():
        q = q_ref[...]
        if mode == "diff":
            lane = lax.broadcasted_iota(I32, q.shape, 1)
            lo = lane < DIFF_DK
            qs_ref[0:tq, :] = (jnp.where(lo, q, 0.0) * scale).astype(BF16)
            qs_ref[tq:2 * tq, :] = (jnp.where(lo, 0.0, q) * scale).astype(BF16)
        else:
            qs_ref[...] = q.astype(BF16)
        m_ref[...] = jnp.full(m_ref.shape, NEG, F32)
        l_ref[...] = jnp.zeros(l_ref.shape, F32)
        acc_ref[...] = jnp.zeros(acc_ref.shape, F32)
        if mode == "fox":
            lane = lax.broadcasted_iota(I32, cq_ref.shape, 1)
            col = jnp.sum(jnp.where(lane == h, cq_ref[...], 0.0), axis=-1, keepdims=True)
            cqs_ref[...] = jnp.broadcast_to(col, cqs_ref.shape)

    def compute():
        s = lax.dot_general(qs_ref[...], k_ref[...].astype(BF16), (((1,), (1,)), ((), ())),
                            preferred_element_type=F32)
        if mode != "diff":
            s = s * scale
        if mode == "fox":
            s = s + (pltpu.repeat(cqs_ref[...], tk // LANES, 1) - ck_ref[0])
        if causal:
            r = lax.broadcasted_iota(I32, s.shape, 0)
            if mode == "diff":
                r = jnp.where(r >= tq, r - tq, r)
            c = lax.broadcasted_iota(I32, s.shape, 1)
            s = jnp.where(ki * tk + c <= qi * tq + r, s, NEG)
        m_prev = m_ref[...]
        m_next = jnp.maximum(m_prev, jnp.max(s, axis=1, keepdims=True))
        alpha = jnp.exp(m_prev - m_next)
        p = jnp.exp(s - pltpu.repeat(m_next, tk // LANES, 1))
        l_ref[...] = alpha * l_ref[...] + jnp.sum(p, axis=1, keepdims=True)
        acc_ref[...] = alpha * acc_ref[...] + jnp.dot(p.astype(BF16), v_ref[...].astype(BF16),
                                                      preferred_element_type=F32)
        m_ref[...] = m_next

    if causal:
        pl.when(ki * tk <= qi * tq + tq - 1)(compute)
    else:
        compute()

    @pl.when(ki == nk - 1)
    def _():
        o = acc_ref[...] / l_ref[...]
        if mode == "diff":
            lp = lam_ref[...]
            lam = (jnp.exp(jnp.sum(lp[0:1] * lp[1:2], axis=-1, keepdims=True))
                   - jnp.exp(jnp.sum(lp[2:3] * lp[3:4], axis=-1, keepdims=True)) + lam_init)
            o = o[0:tq] - lam * o[tq:2 * tq]
        o_ref[...] = o


def flash_attention(qa, qc0, ka, kc0, va, vc0, *, bsz, heads, sq, sk, mode, causal, scale, tq, tk,
                    cq=None, ck=None, lam_params=None, lam_init=0.0):
    tq = min(tq, sq)
    tk = min(tk, sk)
    nq, nk = sq // tq, sk // tk
    assert sq % tq == 0 and sk % tk == 0 and tk % LANES == 0
    rows = 2 * tq if mode == "diff" else tq

    def kidx(qi, ki):
        return jnp.minimum(ki, (qi * tq + tq - 1) // tk) if causal else ki

    in_specs = [pl.BlockSpec((tq, LANES), lambda b, h, qi, ki: (b * nq + qi, qc0 + h)),
                pl.BlockSpec((tk, LANES), lambda b, h, qi, ki: (b * nk + kidx(qi, ki), kc0 + h)),
                pl.BlockSpec((tk, LANES), lambda b, h, qi, ki: (b * nk + kidx(qi, ki), vc0 + h))]
    args = [qa, ka, va]
    scratch = [pltpu.VMEM((rows, LANES), BF16), pltpu.VMEM((rows, LANES), F32),
               pltpu.VMEM((rows, LANES), F32), pltpu.VMEM((rows, LANES), F32)]
    if mode == "fox":
        in_specs += [pl.BlockSpec((tq, heads), lambda b, h, qi, ki: (b * nq + qi, 0)),
                     pl.BlockSpec((1, 1, tk), lambda b, h, qi, ki: (b * heads + h, 0, kidx(qi, ki)))]
        args += [cq, ck]
        scratch.append(pltpu.VMEM((tq, LANES), F32))
    elif mode == "diff":
        in_specs.append(pl.BlockSpec(lam_params.shape, lambda b, h, qi, ki: (0, 0)))
        args.append(lam_params)
    return pl.pallas_call(
        functools.partial(_flash_kernel, mode=mode, causal=causal, scale=scale, tq=tq, tk=tk,
                          lam_init=lam_init),
        grid=(bsz, heads, nq, nk),
        in_specs=in_specs,
        out_specs=pl.BlockSpec((tq, LANES), lambda b, h, qi, ki: (b * nq + qi, h)),
        out_shape=jax.ShapeDtypeStruct((bsz * sq, heads * LANES), F32),
        scratch_shapes=scratch,
        compiler_params=_params("parallel", "parallel", "parallel", "arbitrary"),
        name="flash_" + mode,
    )(*args)


def _split3(x):
    hi = x.astype(BF16)
    r1 = x - hi.astype(F32)
    mid = r1.astype(BF16)
    lo = (r1 - mid.astype(F32)).astype(BF16)
    return hi, mid, lo


def _decode_kernel(*refs, mode, scale, gw, nrow, lam_init):
    if mode == "fox":
        (pt_ref, q_ref, kn_ref, vn_ref, kc_ref, vc_ref, lfc_ref, lfn_ref, o_ref,
         qr_ref, m_ref, l_ref, acc_ref, carry_ref) = refs
    else:
        (pt_ref, q_ref, kn_ref, vn_ref, kc_ref, vc_ref, lam_ref, o_ref,
         qr_ref, m_ref, l_ref, acc_ref) = refs
    j = pl.program_id(1)
    npages = pl.num_programs(1)
    w = q_ref.shape[2]
    wv = vn_ref.shape[2]
    vshift = 0 if mode == "fox" else 1

    @pl.when(j == 0)
    def _():
        q = q_ref[0]
        rid = lax.broadcasted_iota(I32, (nrow, w), 0)
        lane = lax.broadcasted_iota(I32, (nrow, w), 1)
        qr = jnp.where((lane >> int(math.log2(gw))) == rid, jnp.broadcast_to(q, (nrow, w)), 0.0)
        s_new = jnp.sum(qr * kn_ref[0], axis=-1, keepdims=True) * scale
        qr_ref[...] = qr.astype(BF16)
        m_ref[...] = jnp.broadcast_to(s_new, m_ref.shape)
        l_ref[...] = jnp.ones(l_ref.shape, F32)
        acc_ref[...] = jnp.broadcast_to(vn_ref[0], acc_ref.shape)
        if mode == "fox":
            carry_ref[...] = jnp.broadcast_to(lfn_ref[0], carry_ref.shape)

    s = lax.dot_general(qr_ref[...], kc_ref[0].astype(BF16), (((1,), (1,)), ((), ())),
                        preferred_element_type=F32) * scale
    if mode == "fox":
        lf = lfc_ref[0]
        kr = lax.broadcasted_iota(I32, (PAGE, PAGE), 0)
        kc = lax.broadcasted_iota(I32, (PAGE, PAGE), 1)
        tri = jnp.where(kr > kc, 1.0, 0.0).astype(BF16)
        hi, mid, lo = _split3(lf)
        suffix = (jnp.dot(hi, tri, preferred_element_type=F32)
                  + jnp.dot(mid, tri, preferred_element_type=F32)
                  + jnp.dot(lo, tri, preferred_element_type=F32))
        s = s + (suffix + carry_ref[...])
        carry_ref[...] = carry_ref[...] + jnp.sum(lf, axis=-1, keepdims=True)
    m_prev = m_ref[...]
    m_next = jnp.maximum(m_prev, jnp.max(s, axis=1, keepdims=True))
    alpha = jnp.exp(m_prev - m_next)
    p = jnp.exp(s - m_next)
    l_ref[...] = alpha * l_ref[...] + jnp.sum(p, axis=1, keepdims=True)
    acc_ref[...] = (pltpu.repeat(alpha, wv // LANES, 1) * acc_ref[...]
                    + jnp.dot(p.astype(BF16), vc_ref[0].astype(BF16), preferred_element_type=F32))
    m_ref[...] = m_next

    @pl.when(j == npages - 1)
    def _():
        inv = 1.0 / l_ref[...]
        if mode == "diff":
            lp = lam_ref[...]
            lam = (jnp.exp(jnp.sum(lp[0:1] * lp[1:2], axis=-1, keepdims=True))
                   - jnp.exp(jnp.sum(lp[2:3] * lp[3:4], axis=-1, keepdims=True)) + lam_init)
            rid = lax.broadcasted_iota(I32, inv.shape, 0)
            inv = jnp.where((rid & 1) == 0, inv, -lam * inv)
        rid = lax.broadcasted_iota(I32, (nrow, wv), 0)
        lane = lax.broadcasted_iota(I32, (nrow, wv), 1)
        sel = (lane >> int(math.log2(LANES))) == (rid >> vshift)
        o = jnp.where(sel, acc_ref[...] * pltpu.repeat(inv, wv // LANES, 1), 0.0)
        o_ref[0] = jnp.sum(o, axis=0, keepdims=True)


def decode_attention(q, k_new, v_new, k_cache, v_cache, page_table, *, mode, scale, gw,
                     lf_cache_t=None, lf_new=None, lam_params=None, lam_init=0.0):
    bsz, w = q.shape
    wv = v_new.shape[1]
    npages = page_table.shape[1]
    nrow = w // gw
    q3, kn3, vn3 = q.reshape(bsz, 1, w), k_new.reshape(bsz, 1, w), v_new.reshape(bsz, 1, wv)
    pt = page_table.reshape(-1).astype(I32)

    def page(b, j, pt_ref):
        return pt_ref[b * npages + (npages - 1 - j)]

    in_specs = [pl.BlockSpec((1, 1, w), lambda b, j, pt_ref: (b, 0, 0)),
                pl.BlockSpec((1, 1, w), lambda b, j, pt_ref: (b, 0, 0)),
                pl.BlockSpec((1, 1, wv), lambda b, j, pt_ref: (b, 0, 0)),
                pl.BlockSpec((1, PAGE, w), lambda b, j, pt_ref: (page(b, j, pt_ref), 0, 0)),
                pl.BlockSpec((1, PAGE, wv), lambda b, j, pt_ref: (page(b, j, pt_ref), 0, 0))]
    args = [q3, kn3, vn3, k_cache, v_cache]
    scratch = [pltpu.VMEM((nrow, w), BF16), pltpu.VMEM((nrow, LANES), F32),
               pltpu.VMEM((nrow, LANES), F32), pltpu.VMEM((nrow, wv), F32)]
    if mode == "fox":
        in_specs += [pl.BlockSpec((1, nrow, PAGE), lambda b, j, pt_ref: (page(b, j, pt_ref), 0, 0)),
                     pl.BlockSpec((1, nrow, 1), lambda b, j, pt_ref: (b, 0, 0))]
        args += [lf_cache_t, lf_new.reshape(bsz, nrow, 1)]
        scratch.append(pltpu.VMEM((nrow, LANES), F32))
    else:
        in_specs.append(pl.BlockSpec(lam_params.shape, lambda b, j, pt_ref: (0, 0)))
        args.append(lam_params)
    out = pl.pallas_call(
        functools.partial(_decode_kernel, mode=mode, scale=scale, gw=gw, nrow=nrow, lam_init=lam_init),
        grid_spec=pltpu.PrefetchScalarGridSpec(
            num_scalar_prefetch=1,
            grid=(bsz, npages),
            in_specs=in_specs,
            out_specs=pl.BlockSpec((1, 1, wv), lambda b, j, pt_ref: (b, 0, 0)),
            scratch_shapes=scratch),
        out_shape=jax.ShapeDtypeStruct((bsz, 1, wv), F32),
        compiler_params=_params("parallel", "arbitrary"),
        name="decode_" + mode,
    )(pt, *args)
    return out.reshape(bsz, wv)


def _cumsum_kernel(x_ref, o_ref, carry_ref, *, tt):
    @pl.when(pl.program_id(1) == 0)
    def _():
        carry_ref[...] = jnp.zeros(carry_ref.shape, F32)

    x = x_ref[0]
    r = lax.broadcasted_iota(I32, (tt, tt), 0)
    c = lax.broadcasted_iota(I32, (tt, tt), 1)
    tri = jnp.where(c <= r, 1.0, 0.0).astype(BF16)
    hi, mid, lo = _split3(x)
    cs = (jnp.dot(tri, hi, preferred_element_type=F32) + jnp.dot(tri, mid, preferred_element_type=F32)
          + jnp.dot(tri, lo, preferred_element_type=F32)) + carry_ref[...]
    o_ref[0] = cs
    carry_ref[...] = cs[tt - 1:tt, :]


def cumsum_time(x, tt=256):
    bsz, s, h = x.shape
    tt = min(tt, s)
    return pl.pallas_call(
        functools.partial(_cumsum_kernel, tt=tt),
        grid=(bsz, s // tt),
        in_specs=[pl.BlockSpec((1, tt, h), lambda b, i: (b, i, 0))],
        out_specs=pl.BlockSpec((1, tt, h), lambda b, i: (b, i, 0)),
        out_shape=jax.ShapeDtypeStruct((bsz, s, h), F32),
        scratch_shapes=[pltpu.VMEM((1, h), F32)],
        compiler_params=_params("parallel", "arbitrary"),
        name="cumsum_time",
    )(x)


def _logsig_kernel(z_ref, b_ref, o_ref):
    o_ref[...] = jax.nn.log_sigmoid(z_ref[...] + b_ref[...])


def log_sigmoid_bias(z, bias, tm=1024):
    m, n = z.shape
    tm = min(tm, m)
    return pl.pallas_call(
        _logsig_kernel,
        grid=(m // tm,),
        in_specs=[pl.BlockSpec((tm, n), lambda i: (i, 0)), pl.BlockSpec((1, n), lambda i: (0, 0))],
        out_specs=pl.BlockSpec((tm, n), lambda i: (i, 0)),
        out_shape=jax.ShapeDtypeStruct((m, n), F32),
        compiler_params=_params("parallel"),
        name="log_sigmoid_bias",
    )(z, bias.reshape(1, n).astype(F32))


_NTOP = PEER_TOPK + 1
_STAIR = [(a, _NTOP // (a + 1)) for a in range(_NTOP)]
_STAIR_ROWS = ((sum(nb for _, nb in _STAIR) + 7) // 8) * 8


def _peer_score_kernel(q_ref, keys_ref, s1_ref, e1_ref, thr_ref, e2_ref, top_ref, cand_ref, cv_ref):
    q = q_ref[...]
    top_ref[...] = jnp.full(top_ref.shape, -jnp.inf, F32)
    cv_ref[...] = jnp.full(cv_ref.shape, -jnp.inf, F32)
    sc = []
    for c in range(2):
        qc = q[:, c * PEER_NKEYS:(c + 1) * PEER_NKEYS].astype(BF16)
        kc = keys_ref[c].astype(BF16)
        s = lax.dot_general(kc, qc, (((1,), (1,)), ((), ())), preferred_element_type=F32)
        sc.append(s)
        for k in range(_NTOP):
            mx = jnp.max(s, axis=0, keepdims=True)
            top_ref[c, k:k + 1, :] = mx
            s = jnp.where(s == mx, -jnp.inf, s)
    v1 = top_ref[0]
    v2 = top_ref[1]
    cand_ref[...] = jnp.full(cand_ref.shape, -jnp.inf, F32)
    off = 0
    for a, nb in _STAIR:
        cand_ref[off:off + nb, :] = v1[a:a + 1, :] + v2[0:nb, :]
        off += nb
    cand = cand_ref[...]
    for k in range(_NTOP):
        mx = jnp.max(cand, axis=0, keepdims=True)
        cv_ref[k:k + 1, :] = mx
        cand = jnp.where(cand == mx, -jnp.inf, cand)
    cv = cv_ref[...]
    tau = 0.5 * (cv[PEER_TOPK - 1:PEER_TOPK, :] + cv[PEER_TOPK:PEER_TOPK + 1, :])
    z = jnp.sum(jnp.exp(cv[0:PEER_TOPK, :] - cv[0:1, :]), axis=0, keepdims=True)
    s1, s2 = sc
    s1_ref[0] = s1
    e1_ref[0] = jnp.exp(s1 - v1[0:1, :])
    thr_ref[0] = tau - s2
    e2_ref[0] = jnp.exp(s2 - v2[0:1, :]) / z


def peer_scores(q, keys, tm=256):
    t = q.shape[0]
    tm = min(tm, t)
    assert t % tm == 0 and tm % LANES == 0
    shp = jax.ShapeDtypeStruct((PEER_HEADS, PEER_NKEYS, t), F32)
    ospec = pl.BlockSpec((1, PEER_NKEYS, tm), lambda i, h: (h, 0, i))
    return pl.pallas_call(
        _peer_score_kernel,
        grid=(t // tm, PEER_HEADS),
        in_specs=[pl.BlockSpec((tm, 2 * PEER_NKEYS), lambda i, h: (i, h)),
                  pl.BlockSpec(keys.shape, lambda i, h: (0, 0, 0))],
        out_specs=[ospec, ospec, ospec, ospec],
        out_shape=[shp, shp, shp, shp],
        scratch_shapes=[pltpu.VMEM((2, _NTOP + 7, tm), F32), pltpu.VMEM((_STAIR_ROWS, tm), F32),
                        pltpu.VMEM((_NTOP + 7, tm), F32)],
        compiler_params=_params("parallel", "parallel"),
        name="peer_scores",
    )(q, keys)


def _peer_dense_kernel(x_ref, g_ref, u_ref, vt_ref, s1_ref, e1_ref, thr_ref, e2_ref, o_ref,
                       xn_ref, at_ref, acc_ref, *, ci):
    c = pl.program_id(1)
    nc = pl.num_programs(1)

    @pl.when(c == 0)
    def _():
        x = x_ref[...]
        xn_ref[...] = (x * lax.rsqrt(jnp.mean(x * x, axis=-1, keepdims=True) + EPS) * g_ref[...]).astype(BF16)
        acc_ref[...] = jnp.zeros(acc_ref.shape, F32)

    ht = lax.dot_general(u_ref[...], xn_ref[...], (((1,), (1,)), ((), ())),
                         preferred_element_type=F32)
    for i in range(ci):
        row = c * ci + i
        gate = jnp.zeros((PEER_NKEYS, x_ref.shape[0]), F32)
        for h in range(PEER_HEADS):
            s1 = s1_ref[h, pl.ds(row, 1), :]
            e1 = e1_ref[h, pl.ds(row, 1), :]
            gate = gate + jnp.where(s1 >= thr_ref[h], e2_ref[h], 0.0) * e1
        hh = ht[i * PEER_NKEYS:(i + 1) * PEER_NKEYS, :]
        at_ref[i * PEER_NKEYS:(i + 1) * PEER_NKEYS, :] = (gate * jax.nn.gelu(hh)).astype(BF16)
    acc_ref[...] += jnp.dot(vt_ref[...], at_ref[...], preferred_element_type=F32)

    @pl.when(c == nc - 1)
    def _():
        o_ref[...] = x_ref[...] + acc_ref[...].T


def peer_dense(x, gain, u_bf, vt_bf, s1, e1, thr, e2, tm=512, ci=4):
    t, d = x.shape
    ne = u_bf.shape[0]
    tm = min(tm, t)
    ec = ci * PEER_NKEYS
    assert t % tm == 0 and ne % ec == 0
    sspec = pl.BlockSpec((PEER_HEADS, PEER_NKEYS, tm), lambda i, c: (0, 0, i))
    return pl.pallas_call(
        functools.partial(_peer_dense_kernel, ci=ci),
        grid=(t // tm, ne // ec),
        in_specs=[pl.BlockSpec((tm, d), lambda i, c: (i, 0)),
                  pl.BlockSpec((1, d), lambda i, c: (0, 0)),
                  pl.BlockSpec((ec, d), lambda i, c: (c, 0)),
                  pl.BlockSpec((d, ec), lambda i, c: (0, c)),
                  sspec, sspec, sspec, sspec],
        out_specs=pl.BlockSpec((tm, d), lambda i, c: (i, 0)),
        out_shape=jax.ShapeDtypeStruct((t, d), F32),
        scratch_shapes=[pltpu.VMEM((tm, d), BF16), pltpu.VMEM((ec, tm), BF16), pltpu.VMEM((d, tm), F32)],
        compiler_params=_params("parallel", "arbitrary"),
        name="peer_dense",
    )(x, gain.reshape(1, d).astype(F32), u_bf, vt_bf, s1, e1, thr, e2)


def peer_layer(x, gain, wq_bf, keys, u_bf, vt_bf):
    t = x.shape[0]
    tp = -(-t // LANES) * LANES
    xp = jnp.pad(x, ((0, tp - t), (0, 0))) if tp != t else x
    q = rms_matmul(xp, gain, wq_bf)
    s1, e1, thr, e2 = peer_scores(q, keys)
    out = peer_dense(xp, gain, u_bf, vt_bf, s1, e1, thr, e2)
    return out[:t]


def kernel(x_prompt, x_sample, mem_prompt, cache_diff_k, cache_diff_v, cache_fox_k, cache_fox_v, cache_fox_logf, state_conv, cache_mem_k, cache_mem_v, page_table, norm_mix, norm_xattn, norm_ffn, even_w_in, even_w_out, diff_q_gain, diff_k_gain, diff_lambda, diff_out_gain, conv_w, conv_b, conv_ln_g, conv_ln_b, odd_w_in, fox_forget_bias, fox_q_gain, fox_k_gain, odd_w_out, mem_norm, xattn_wq, xattn_wk, xattn_wv, xattn_wo, xattn_q_gain, xattn_k_gain, peer_wq, peer_keys, peer_u, peer_v):
    bp, seq, d = x_prompt.shape
    bs = x_sample.shape[0]
    depth = norm_mix.shape[0]
    half = d // 2
    n_pages = page_table.shape[1]
    past = n_pages * PAGE
    n_phys = cache_diff_k.shape[1]
    mem_len = mem_prompt.shape[1]
    xw = xattn_wq.shape[2]
    tp = bp * seq
    srow = 8

    xp = x_prompt.reshape(tp, d)
    xs = x_sample.reshape(bs, d)
    memf = mem_prompt.reshape(bp * mem_len, d)

    rope_p = rope_tables(jnp.arange(seq, dtype=I32), seq)
    rope_s = rope_tables(jnp.full((1,), past, I32), bs)

    outs = {k: [] for k in ("dk_p", "dv_p", "dk_s", "dv_s", "fk_p", "fv_p", "fl_p", "fk_s", "fv_s", "fl_s",
                            "cv_p", "cv_s", "mk_p", "mv_p")}

    for l in range(depth):
        i = l // 2
        if l % 2 == 0:
            lam_init = 0.8 - 0.6 * math.exp(-0.3 * l)
            w_in = even_w_in[i].astype(BF16)
            w_out = even_w_out[i].astype(BF16)
            nq = half // LANES
            res = []
            for grp, (x, rope, rows) in enumerate(((xp, rope_p, tp), (xs, rope_s, bs))):
                z = rms_matmul(x, norm_mix[l], w_in)
                qn = head_norm(z, 0, nq, diff_q_gain[i], DIFF_DK, rope)
                kn = head_norm(z, nq, nq, diff_k_gain[i], DIFF_DK, rope)
                v = z[:, 2 * half:3 * half]
                u = glu(z, 3 * half, 4 * half, half)
                res.append((z, qn, kn, v, u))
            z, qn, kn, v, u = res[0]
            attn = flash_attention(qn, 0, kn, 0, z, 2 * nq, bsz=bp, heads=DIFF_HEADS, sq=seq, sk=seq,
                                   mode="diff", causal=True, scale=DIFF_DK ** -0.5, tq=256, tk=512,
                                   lam_params=diff_lambda[i], lam_init=lam_init)
            attn = head_norm(attn, 0, nq, diff_out_gain[i], LANES, out_scale=1.0 - lam_init)
            u3 = u.reshape(bp, seq, half)
            conv = conv_branch(u3, jnp.zeros((bp, CONV_HALO, half), F32), conv_w[i], conv_b[i],
                               conv_ln_g[i], conv_ln_b[i]).reshape(tp, half)
            xp = rms_matmul(jnp.concatenate([attn, conv], axis=1), None, w_out, res=xp)
            outs["dk_p"].append(kn.reshape(bp, seq, DIFF_HEADS, 2, DIFF_DK))
            outs["dv_p"].append(v.reshape(bp, seq, DIFF_HEADS, 2 * DIFF_DK))
            outs["cv_p"].append(u3[:, seq - (CONV_WIDTH - 1):])
            z, qn, kn, v, u = res[1]
            attn = decode_attention(qn, kn, v, cache_diff_k[i].reshape(n_phys, PAGE, half),
                                    cache_diff_v[i].reshape(n_phys, PAGE, half), page_table,
                                    mode="diff", scale=DIFF_DK ** -0.5, gw=DIFF_DK,
                                    lam_params=diff_lambda[i], lam_init=lam_init)
            attn = head_norm(attn, 0, nq, diff_out_gain[i], LANES, out_scale=1.0 - lam_init)
            st = state_conv[i]
            init = jnp.concatenate([jnp.zeros((bs, CONV_HALO - (CONV_WIDTH - 1), half), F32), st], axis=1)
            u_pad = jnp.concatenate([u[:, None, :], jnp.zeros((bs, srow - 1, half), F32)], axis=1)
            conv = conv_branch(u_pad, init, conv_w[i], conv_b[i], conv_ln_g[i], conv_ln_b[i])[:, 0]
            xs = rms_matmul(jnp.concatenate([attn, conv], axis=1), None, w_out, res=xs)
            outs["dk_s"].append(kn.reshape(bs, 1, DIFF_HEADS, 2, DIFF_DK))
            outs["dv_s"].append(v.reshape(bs, 1, DIFF_HEADS, 2 * DIFF_DK))
            outs["cv_s"].append(jnp.concatenate([st[:, 1:], u[:, None, :]], axis=1))
        else:
            w_in = odd_w_in[i]
            w_main = w_in[:, :3 * d].astype(BF16)
            w_gate = jnp.pad(w_in[:, 3 * d:], ((0, 0), (0, LANES - FOX_HEADS))).astype(BF16)
            w_out = odd_w_out[i].astype(BF16)
            nh = d // LANES
            fbias = jnp.pad(fox_forget_bias[i], (0, LANES - FOX_HEADS))
            res = []
            for x in (xp, xs):
                z = rms_matmul(x, norm_mix[l], w_main)
                zg = rms_matmul(x, norm_mix[l], w_gate)
                qn = head_norm(z, 0, nh, fox_q_gain[i], LANES)
                kn = head_norm(z, nh, nh, fox_k_gain[i], LANES)
                v = z[:, 2 * d:3 * d]
                logf = log_sigmoid_bias(zg, fbias)[:, :FOX_HEADS]
                res.append((z, qn, kn, v, logf))
            z, qn, kn, v, logf = res[0]
            cum = cumsum_time(logf.reshape(bp, seq, FOX_HEADS))
            cum_t = jnp.swapaxes(cum, 1, 2).reshape(bp * FOX_HEADS, 1, seq)
            attn = flash_attention(qn, 0, kn, 0, z, 2 * nh, bsz=bp, heads=FOX_HEADS, sq=seq, sk=seq,
                                   mode="fox", causal=True, scale=LANES ** -0.5, tq=512, tk=512,
                                   cq=cum.reshape(tp, FOX_HEADS), ck=cum_t)
            xp = rms_matmul(attn, None, w_out, res=xp)
            outs["fk_p"].append(kn.reshape(bp, seq, FOX_HEADS, LANES))
            outs["fv_p"].append(v.reshape(bp, seq, FOX_HEADS, LANES))
            outs["fl_p"].append(logf.reshape(bp, seq, FOX_HEADS))
            z, qn, kn, v, logf = res[1]
            lf_cache_t = jnp.swapaxes(cache_fox_logf[i], 1, 2)
            attn = decode_attention(qn, kn, v, cache_fox_k[i].reshape(n_phys, PAGE, d),
                                    cache_fox_v[i].reshape(n_phys, PAGE, d), page_table,
                                    mode="fox", scale=LANES ** -0.5, gw=LANES,
                                    lf_cache_t=lf_cache_t, lf_new=logf)
            xs = rms_matmul(attn, None, w_out, res=xs)
            outs["fk_s"].append(kn.reshape(bs, 1, FOX_HEADS, LANES))
            outs["fv_s"].append(v.reshape(bs, 1, FOX_HEADS, LANES))
            outs["fl_s"].append(logf.reshape(bs, 1, FOX_HEADS))

        wq = xattn_wq[l].astype(BF16)
        wo = xattn_wo[l].astype(BF16)
        wkv = jnp.concatenate([xattn_wk[l], xattn_wv[l]], axis=1).astype(BF16)
        nxh = xw // LANES
        kv = rms_matmul(memf, mem_norm[l], wkv)
        mk = head_norm(kv, 0, nxh, xattn_k_gain[l], LANES)
        mv = kv[:, xw:]
        outs["mk_p"].append(mk.reshape(bp, mem_len, X_HEADS, LANES))
        outs["mv_p"].append(mv.reshape(bp, mem_len, X_HEADS, LANES))
        xq = head_norm(rms_matmul(xp, norm_xattn[l], wq), 0, nxh, xattn_q_gain[l], LANES)
        o = flash_attention(xq, 0, mk, 0, kv, nxh, bsz=bp, heads=X_HEADS, sq=seq, sk=mem_len,
                            mode="plain", causal=False, scale=LANES ** -0.5, tq=1024, tk=mem_len)
        xp = rms_matmul(o, None, wo, res=xp)
        xq = head_norm(rms_matmul(xs, norm_xattn[l], wq), 0, nxh, xattn_q_gain[l], LANES)
        xq = jnp.pad(xq[:, None, :], ((0, 0), (0, srow - 1), (0, 0))).reshape(bs * srow, xw)
        o = flash_attention(xq, 0, cache_mem_k[l].reshape(bs * mem_len, xw), 0,
                            cache_mem_v[l].reshape(bs * mem_len, xw), 0, bsz=bs, heads=X_HEADS,
                            sq=srow, sk=mem_len, mode="plain", causal=False, scale=LANES ** -0.5,
                            tq=srow, tk=mem_len)
        o = o.reshape(bs, srow, xw)[:, 0]
        xs = rms_matmul(o, None, wo, res=xs)

        pwq = peer_wq[l].astype(BF16)
        u_bf = peer_u[l].astype(BF16)
        vt_bf = peer_v[l].astype(BF16).T
        xp = peer_layer(xp, norm_ffn[l], pwq, peer_keys[l], u_bf, vt_bf)
        xs = peer_layer(xs, norm_ffn[l], pwq, peer_keys[l], u_bf, vt_bf)

    st = lambda k: jnp.stack(outs[k])
    return (xp.reshape(bp, seq, d), xs.reshape(bs, 1, d),
            st("dk_p"), st("dv_p"), st("fk_p"), st("fv_p"), st("fl_p"),
            st("cv_p"), st("mk_p"), st("mv_p"),
            st("dk_s"), st("dv_s"), st("fk_s"), st("fv_s"), st("fl_s"), st("cv_s"))
```

```python
import functools
import math

import jax
import jax.numpy as jnp
from jax import lax
from jax.experimental import pallas as pl
from jax.experimental.pallas import tpu as pltpu

F32 = jnp.float32
BF16 = jnp.bfloat16
I32 = jnp.int32

EPS = 1e-6
NEG = -1e30
ROPE_THETA = 500000.0
LANES = 128
V7X_VMEM_BYTES = 64 * 1024 * 1024
VMEM_LIMIT = V7X_VMEM_BYTES - 8 * 1024 * 1024

DIFF_HEADS = 8
DIFF_DK = 64
DIFF_ROT = 16
FOX_HEADS = 16
X_HEADS = 4
PEER_HEADS = 8
PEER_NKEYS = 128
PEER_TOPK = 16
CONV_WIDTH = 31
CONV_HALO = 32
PAGE = 128


def _params(*sem):
    return pltpu.CompilerParams(dimension_semantics=sem, vmem_limit_bytes=VMEM_LIMIT)


def _rms_matmul_kernel(*refs, use_norm, has_res):
    if has_res:
        x_ref, g_ref, w_ref, r_ref, o_ref, xn_ref = refs
    else:
        x_ref, g_ref, w_ref, o_ref, xn_ref = refs

    @pl.when(pl.program_id(1) == 0)
    def _():
        x = x_ref[...]
        if use_norm:
            x = x * lax.rsqrt(jnp.mean(x * x, axis=-1, keepdims=True) + EPS) * g_ref[...]
        xn_ref[...] = x.astype(BF16)

    acc = jnp.dot(xn_ref[...], w_ref[...], preferred_element_type=F32)
    if has_res:
        acc = r_ref[...] + acc
    o_ref[...] = acc


def rms_matmul(x, gain, w, res=None, tm=1024, tn=512):
    m, k = x.shape
    n = w.shape[1]
    tm = min(tm, m)
    tn = min(tn, n)
    assert m % tm == 0 and n % tn == 0, (m, n, tm, tn)
    use_norm = gain is not None
    g = (gain if use_norm else jnp.ones((k,), F32)).reshape(1, k).astype(F32)
    in_specs = [pl.BlockSpec((tm, k), lambda i, j: (i, 0)),
                pl.BlockSpec((1, k), lambda i, j: (0, 0)),
                pl.BlockSpec((k, tn), lambda i, j: (0, j))]
    args = [x, g, w]
    if res is not None:
        in_specs.append(pl.BlockSpec((tm, tn), lambda i, j: (i, j)))
        args.append(res)
    return pl.pallas_call(
        functools.partial(_rms_matmul_kernel, use_norm=use_norm, has_res=res is not None),
        grid=(m // tm, n // tn),
        in_specs=in_specs,
        out_specs=pl.BlockSpec((tm, tn), lambda i, j: (i, j)),
        out_shape=jax.ShapeDtypeStruct((m, n), F32),
        scratch_shapes=[pltpu.VMEM((tm, k), BF16)],
        compiler_params=_params("parallel", "arbitrary"),
        name="rms_matmul",
    )(*args)


def _head_norm_kernel(*refs, gw, rope, out_scale):
    if rope:
        a_ref, g_ref, cos_ref, sin_ref, o_ref = refs
    else:
        a_ref, g_ref, o_ref = refs
    x = a_ref[...]
    xx = x * x
    lane = lax.broadcasted_iota(I32, x.shape, 1)
    if gw == LANES:
        ms = jnp.mean(xx, axis=-1, keepdims=True)
    else:
        lo = lane < gw
        s_lo = jnp.sum(jnp.where(lo, xx, 0.0), axis=-1, keepdims=True)
        s_hi = jnp.sum(jnp.where(lo, 0.0, xx), axis=-1, keepdims=True)
        ms = jnp.where(lo, s_lo, s_hi) * (1.0 / gw)
    y = x * lax.rsqrt(ms + EPS) * g_ref[...]
    if rope:
        half = DIFF_ROT // 2
        first = (lane & (gw - 1)) < half
        partner = jnp.where(first, pltpu.roll(y, LANES - half, 1), pltpu.roll(y, half, 1))
        y = y * cos_ref[...] + partner * sin_ref[...]
    if out_scale != 1.0:
        y = y * out_scale
    o_ref[...] = y


def head_norm(a, col0, ncols, gain, gw, rope_tabs=None, out_scale=1.0, tm=1024):
    m = a.shape[0]
    tm = min(tm, m)
    if rope_tabs is not None:
        tm = min(tm, rope_tabs[0].shape[0])
    assert m % tm == 0
    g = jnp.tile(gain.astype(F32), LANES // gw).reshape(1, LANES)
    in_specs = [pl.BlockSpec((tm, LANES), lambda i, j: (i, col0 + j)),
                pl.BlockSpec((1, LANES), lambda i, j: (0, 0))]
    args = [a, g]
    if rope_tabs is not None:
        cos_t, sin_t = rope_tabs
        nrep = cos_t.shape[0] // tm
        in_specs += [pl.BlockSpec((tm, LANES), lambda i, j: (i % nrep, 0)),
                     pl.BlockSpec((tm, LANES), lambda i, j: (i % nrep, 0))]
        args += [cos_t, sin_t]
    return pl.pallas_call(
        functools.partial(_head_norm_kernel, gw=gw, rope=rope_tabs is not None, out_scale=out_scale),
        grid=(m // tm, ncols),
        in_specs=in_specs,
        out_specs=pl.BlockSpec((tm, LANES), lambda i, j: (i, j)),
        out_shape=jax.ShapeDtypeStruct((m, ncols * LANES), F32),
        compiler_params=_params("parallel", "parallel"),
        name="head_norm",
    )(*args)


def rope_tables(pos, rows):
    half = DIFF_ROT // 2
    inv = jnp.power(ROPE_THETA, -jnp.arange(half, dtype=F32) * 2.0 / DIFF_ROT)
    ang = pos.astype(F32)[:, None] * inv[None, :]
    cos, sin = jnp.cos(ang), jnp.sin(ang)
    ones = jnp.ones((pos.shape[0], DIFF_DK - DIFF_ROT), F32)
    cos64 = jnp.concatenate([cos, cos, ones], axis=1)
    sin64 = jnp.concatenate([-sin, sin, 0.0 * ones], axis=1)
    cos_t = jnp.tile(cos64, (1, LANES // DIFF_DK))
    sin_t = jnp.tile(sin64, (1, LANES // DIFF_DK))
    if cos_t.shape[0] != rows:
        cos_t = jnp.broadcast_to(cos_t[:1], (rows, LANES))
        sin_t = jnp.broadcast_to(sin_t[:1], (rows, LANES))
    return cos_t, sin_t


def _glu_kernel(a_ref, b_ref, o_ref):
    o_ref[...] = a_ref[...] * jax.nn.sigmoid(b_ref[...])


def glu(z, cola, colb, width, tm=1024, tn=512):
    m = z.shape[0]
    tm = min(tm, m)
    nb = width // tn
    return pl.pallas_call(
        _glu_kernel,
        grid=(m // tm, nb),
        in_specs=[pl.BlockSpec((tm, tn), lambda i, j: (i, cola // tn + j)),
                  pl.BlockSpec((tm, tn), lambda i, j: (i, colb // tn + j))],
        out_specs=pl.BlockSpec((tm, tn), lambda i, j: (i, j)),
        out_shape=jax.ShapeDtypeStruct((m, width), F32),
        compiler_params=_params("parallel", "parallel"),
        name="glu",
    )(z, z)


def _conv_kernel(cur_ref, prev_ref, init_ref, w_ref, b_ref, g_ref, beta_ref, o_ref, buf_ref, *, tt, rc):
    i = pl.program_id(1)

    @pl.when(i == 0)
    def _():
        buf_ref[0:CONV_HALO, :] = init_ref[0]

    if tt >= CONV_HALO:
        @pl.when(i > 0)
        def _():
            buf_ref[0:CONV_HALO, :] = prev_ref[0, tt - CONV_HALO:tt, :]

    buf_ref[CONV_HALO:CONV_HALO + tt, :] = cur_ref[0]
    off = CONV_HALO - (CONV_WIDTH - 1)
    for r0 in range(0, tt, rc):
        acc = jnp.zeros((rc, cur_ref.shape[2]), F32)
        for j in range(CONV_WIDTH):
            acc = acc + w_ref[j:j + 1, :] * buf_ref[r0 + off + j:r0 + off + j + rc, :]
        y = acc + b_ref[...]
        mu = jnp.mean(y, axis=-1, keepdims=True)
        yc = y - mu
        var = jnp.mean(yc * yc, axis=-1, keepdims=True)
        y = yc * lax.rsqrt(var + EPS) * g_ref[...] + beta_ref[...]
        o_ref[0, r0:r0 + rc, :] = y * jax.nn.sigmoid(y)


def conv_branch(u, init, w, b, ln_g, ln_b, tt=128):
    bsz, length, ch = u.shape
    tt = min(tt, length)
    assert length % tt == 0 and (tt >= CONV_HALO or length == tt)
    rc = min(16, tt)
    wpad = jnp.concatenate([w, jnp.zeros((CONV_HALO - CONV_WIDTH, ch), F32)], axis=0)
    row = lambda v: v.reshape(1, ch).astype(F32)
    return pl.pallas_call(
        functools.partial(_conv_kernel, tt=tt, rc=rc),
        grid=(bsz, length // tt),
        in_specs=[pl.BlockSpec((1, tt, ch), lambda bi, i: (bi, i, 0)),
                  pl.BlockSpec((1, tt, ch), lambda bi, i: (bi, jnp.maximum(i - 1, 0), 0)),
                  pl.BlockSpec((1, CONV_HALO, ch), lambda bi, i: (bi, 0, 0)),
                  pl.BlockSpec((CONV_HALO, ch), lambda bi, i: (0, 0)),
                  pl.BlockSpec((1, ch), lambda bi, i: (0, 0)),
                  pl.BlockSpec((1, ch), lambda bi, i: (0, 0)),
                  pl.BlockSpec((1, ch), lambda bi, i: (0, 0))],
        out_specs=pl.BlockSpec((1, tt, ch), lambda bi, i: (bi, i, 0)),
        out_shape=jax.ShapeDtypeStruct((bsz, length, ch), F32),
        scratch_shapes=[pltpu.VMEM((CONV_HALO + tt, ch), F32)],
        compiler_params=_params("parallel", "arbitrary"),
        name="conv_branch",
    )(u, u, init, wpad, row(b), row(ln_g), row(ln_b))


LOG2E = 1.4426950408889634


def _flash_kernel(*refs, mode, causal, qscale, tq, tk, lam_init):
    qi_ref, ki_ref, last_ref = refs[:3]
    if mode == "fox":
        q_ref, k_ref, v_ref, ck_ref, o_ref, qs_ref, m_ref, l_ref, acc_ref = refs[3:]
    elif mode == "diff":
        q_ref, k_ref, v_ref, lam_ref, o_ref, qs_ref, m_ref, l_ref, acc_ref = refs[3:]
    else:
        q_ref, k_ref, v_ref, o_ref, qs_ref, m_ref, l_ref, acc_ref = refs[3:]
    step = pl.program_id(2)
    qi = qi_ref[step]
    ki = ki_ref[step]

    @pl.when(ki == 0)
    def _():
        q = q_ref[...] * qscale
        if mode == "diff":
            lane = lax.broadcasted_iota(I32, q.shape, 1)
            lo = lane < DIFF_DK
            qs_ref[0:tq, :] = jnp.where(lo, q, 0.0).astype(BF16)
            qs_ref[tq:2 * tq, :] = jnp.where(lo, 0.0, q).astype(BF16)
        else:
            qs_ref[...] = q.astype(BF16)
        m_ref[...] = jnp.full(m_ref.shape, NEG, F32)
        l_ref[...] = jnp.zeros(l_ref.shape, F32)
        acc_ref[...] = jnp.zeros(acc_ref.shape, F32)

    def compute(masked):
        s = lax.dot_general(qs_ref[...], k_ref[...].astype(BF16), (((1,), (1,)), ((), ())),
                            preferred_element_type=F32)
        if mode == "fox":
            s = s - ck_ref[0] * LOG2E
        if masked:
            r = lax.broadcasted_iota(I32, s.shape, 0)
            if mode == "diff":
                r = jnp.where(r >= tq, r - tq, r)
            c = lax.broadcasted_iota(I32, s.shape, 1)
            s = jnp.where(ki * tk + c <= qi * tq + r, s, NEG)
        m_prev = m_ref[...]
        m_next = jnp.maximum(m_prev, jnp.max(s, axis=1, keepdims=True))
        alpha = jnp.exp2(m_prev - m_next)
        p = jnp.exp2(s - pltpu.repeat(m_next, tk // LANES, 1))
        l_ref[...] = alpha * l_ref[...] + jnp.sum(p, axis=1, keepdims=True)
        acc_ref[...] = alpha * acc_ref[...] + jnp.dot(p.astype(BF16), v_ref[...].astype(BF16),
                                                      preferred_element_type=F32)
        m_ref[...] = m_next

    if causal:
        crosses = ki * tk + tk - 1 > qi * tq
        pl.when(crosses)(functools.partial(compute, True))
        pl.when(jnp.logical_not(crosses))(functools.partial(compute, False))
    else:
        compute(False)

    @pl.when(last_ref[step] == 1)
    def _():
        o = acc_ref[...] / l_ref[...]
        if mode == "diff":
            lp = lam_ref[...]
            lam = (jnp.exp(jnp.sum(lp[0:1] * lp[1:2], axis=-1, keepdims=True))
                   - jnp.exp(jnp.sum(lp[2:3] * lp[3:4], axis=-1, keepdims=True)) + lam_init)
            o = o[0:tq] - lam * o[tq:2 * tq]
        o_ref[...] = o


def flash_attention(qa, qc0, ka, kc0, va, vc0, *, bsz, heads, sq, sk, mode, causal, scale, tq, tk,
                    ck=None, lam_params=None, lam_init=0.0):
    tq = min(tq, sq)
    tk = min(tk, sk)
    nq, nk = sq // tq, sk // tk
    assert sq % tq == 0 and sk % tk == 0 and tk % LANES == 0
    rows = 2 * tq if mode == "diff" else tq
    pairs = []
    for qi in range(nq):
        kmax = (qi * tq + tq - 1) // tk if causal else nk - 1
        pairs += [(qi, ki, int(ki == kmax)) for ki in range(kmax + 1)]
    qi_l, ki_l, last_l = (jnp.asarray(v, I32) for v in zip(*pairs))

    in_specs = [pl.BlockSpec((tq, LANES), lambda b, h, p, qr, kr, lr: (b * nq + qr[p], qc0 + h)),
                pl.BlockSpec((tk, LANES), lambda b, h, p, qr, kr, lr: (b * nk + kr[p], kc0 + h)),
                pl.BlockSpec((tk, LANES), lambda b, h, p, qr, kr, lr: (b * nk + kr[p], vc0 + h))]
    args = [qa, ka, va]
    scratch = [pltpu.VMEM((rows, LANES), BF16), pltpu.VMEM((rows, LANES), F32),
               pltpu.VMEM((rows, LANES), F32), pltpu.VMEM((rows, LANES), F32)]
    if mode == "fox":
        in_specs.append(pl.BlockSpec((1, 1, tk), lambda b, h, p, qr, kr, lr: (b * heads + h, 0, kr[p])))
        args.append(ck)
    elif mode == "diff":
        in_specs.append(pl.BlockSpec(lam_params.shape, lambda b, h, p, qr, kr, lr: (0, 0)))
        args.append(lam_params)
    return pl.pallas_call(
        functools.partial(_flash_kernel, mode=mode, causal=causal, qscale=scale * LOG2E, tq=tq, tk=tk,
                          lam_init=lam_init),
        grid_spec=pltpu.PrefetchScalarGridSpec(
            num_scalar_prefetch=3,
            grid=(bsz, heads, len(pairs)),
            in_specs=in_specs,
            out_specs=pl.BlockSpec((tq, LANES), lambda b, h, p, qr, kr, lr: (b * nq + qr[p], h)),
            scratch_shapes=scratch),
        out_shape=jax.ShapeDtypeStruct((bsz * sq, heads * LANES), F32),
        compiler_params=_params("parallel", "parallel", "arbitrary"),
        name="flash_" + mode,
    )(qi_l, ki_l, last_l, *args)


DEC_ROWS = 16


def _split3(x):
    hi = x.astype(BF16)
    r1 = x - hi.astype(F32)
    mid = r1.astype(BF16)
    lo = (r1 - mid.astype(F32)).astype(BF16)
    return hi, mid, lo


def _decode_kernel(*refs, mode, scale, vheads, pps, lam_init):
    refs = refs[1:]
    if mode == "fox":
        q_ref, kn_ref, vn_ref = refs[:3]
        kc_refs, vc_refs, lf_refs = refs[3:3 + pps], refs[3 + pps:3 + 2 * pps], refs[3 + 2 * pps:3 + 3 * pps]
        lfn_ref, o_ref, qm_ref, m_ref, l_ref, acc_ref, carry_ref = refs[3 + 3 * pps:]
    else:
        q_ref, kn_ref, qrep_ref, vn_ref = refs[:4]
        kc_refs, vc_refs = refs[4:4 + pps], refs[4 + pps:4 + 2 * pps]
        lam_ref, o_ref, m_ref, l_ref, acc_ref = refs[4 + 2 * pps:]
    j = pl.program_id(1)
    nsteps = pl.num_programs(1)
    vshift = 0 if mode == "fox" else 1
    rid = lax.broadcasted_iota(I32, (DEC_ROWS, LANES), 0)

    @pl.when(j == 0)
    def _():
        q = q_ref[0]
        s_new = jnp.sum(q * kn_ref[0], axis=-1, keepdims=True) * scale
        m_ref[...] = jnp.broadcast_to(s_new, m_ref.shape)
        l_ref[...] = jnp.ones(l_ref.shape, F32)
        acc_ref[...] = vn_ref[0]
        if mode == "fox":
            for h in range(DEC_ROWS):
                qm_ref[h] = jnp.where(rid == h, q, 0.0).astype(BF16)
            carry_ref[...] = jnp.broadcast_to(lfn_ref[0], carry_ref.shape)

    for g in range(pps):
        if mode == "fox":
            s = jnp.zeros((DEC_ROWS, PAGE), F32)
            for h in range(DEC_ROWS):
                kh = kc_refs[g][0, pl.ds(h, PAGE, stride=DEC_ROWS), :].astype(BF16)
                s = s + lax.dot_general(qm_ref[h], kh, (((1,), (1,)), ((), ())), preferred_element_type=F32)
            s = s * scale
            lf = lf_refs[g][0]
            kr = lax.broadcasted_iota(I32, (PAGE, PAGE), 0)
            kc = lax.broadcasted_iota(I32, (PAGE, PAGE), 1)
            tri = jnp.where(kr > kc, 1.0, 0.0).astype(BF16)
            hi, mid, lo = _split3(lf)
            suffix = (jnp.dot(hi, tri, preferred_element_type=F32) + jnp.dot(mid, tri, preferred_element_type=F32)
                      + jnp.dot(lo, tri, preferred_element_type=F32))
            s = s + (suffix + carry_ref[...])
            carry_ref[...] = carry_ref[...] + jnp.sum(lf, axis=-1, keepdims=True)
        else:
            s = jnp.sum(kc_refs[g][0] * qrep_ref[0], axis=1) * scale
        m_prev = m_ref[...]
        m_next = jnp.maximum(m_prev, jnp.max(s, axis=1, keepdims=True))
        alpha = jnp.exp(m_prev - m_next)
        p = jnp.exp(s - m_next)
        l_ref[...] = alpha * l_ref[...] + jnp.sum(p, axis=1, keepdims=True)
        pb = p.astype(BF16)
        pv = jnp.zeros((DEC_ROWS, LANES), F32)
        for h in range(vheads):
            vh = vc_refs[g][0, pl.ds(h, PAGE, stride=vheads), :].astype(BF16)
            pv = jnp.where((rid >> vshift) == h, jnp.dot(pb, vh, preferred_element_type=F32), pv)
        acc_ref[...] = alpha * acc_ref[...] + pv
        m_ref[...] = m_next

    @pl.when(j == nsteps - 1)
    def _():
        inv = 1.0 / l_ref[...]
        if mode == "fox":
            o_ref[0] = acc_ref[...] * inv
        else:
            lp = lam_ref[...]
            lam = (jnp.exp(jnp.sum(lp[0:1] * lp[1:2], axis=-1, keepdims=True))
                   - jnp.exp(jnp.sum(lp[2:3] * lp[3:4], axis=-1, keepdims=True)) + lam_init)
            acc_ref[...] = acc_ref[...] * jnp.where((rid & 1) == 0, inv, -lam * inv)
            o_ref[0] = (acc_ref[pl.ds(0, DEC_ROWS // 2, stride=2), :]
                        + acc_ref[pl.ds(1, DEC_ROWS // 2, stride=2), :])


def decode_attention(q, k_new, v_new, k_cache, v_cache, page_table, page_base, *, mode, scale,
                     lf_cache_t=None, lf_new=None, lam_params=None, lam_init=0.0, pps=2):
    bsz = q.shape[0]
    npages = page_table.shape[1]
    assert npages % pps == 0
    pt = page_table.reshape(-1).astype(I32)
    vheads = v_cache.shape[1] // PAGE

    def page(g, ndim):
        return lambda b, j, pt_ref: ((page_base + pt_ref[b * npages + (npages - 1 - (j * pps + g))],)
                                     + (0,) * (ndim - 1))

    per_b = lambda shape: pl.BlockSpec((1,) + shape, lambda b, j, pt_ref: (b,) + (0,) * len(shape))
    kspecs = [pl.BlockSpec((1,) + k_cache.shape[1:], page(g, k_cache.ndim)) for g in range(pps)]
    vspecs = [pl.BlockSpec((1,) + v_cache.shape[1:], page(g, v_cache.ndim)) for g in range(pps)]
    if mode == "fox":
        in_specs = ([per_b(q.shape[1:]), per_b(k_new.shape[1:]), per_b(v_new.shape[1:])] + kspecs + vspecs
                    + [pl.BlockSpec((1, DEC_ROWS, PAGE), page(g, 3)) for g in range(pps)]
                    + [per_b((DEC_ROWS, 1))])
        args = ([q, k_new, v_new] + [k_cache] * pps + [v_cache] * pps + [lf_cache_t] * pps
                + [lf_new.reshape(bsz, DEC_ROWS, 1)])
        scratch = [pltpu.VMEM((DEC_ROWS, DEC_ROWS, LANES), BF16)]
        out_rows = DEC_ROWS
    else:
        qrep = jnp.broadcast_to(q[..., None], q.shape + (PAGE,))
        v_rows = jnp.repeat(v_new, 2, axis=1)
        in_specs = ([per_b(q.shape[1:]), per_b(k_new.shape[1:]), per_b(qrep.shape[1:]), per_b(v_rows.shape[1:])]
                    + kspecs + vspecs + [pl.BlockSpec(lam_params.shape, lambda b, j, pt_ref: (0, 0))])
        args = [q, k_new, qrep, v_rows] + [k_cache] * pps + [v_cache] * pps + [lam_params]
        scratch = []
        out_rows = DEC_ROWS // 2
    scratch += [pltpu.VMEM((DEC_ROWS, LANES), F32), pltpu.VMEM((DEC_ROWS, LANES), F32),
                pltpu.VMEM((DEC_ROWS, LANES), F32)]
    if mode == "fox":
        scratch.append(pltpu.VMEM((DEC_ROWS, LANES), F32))
    return pl.pallas_call(
        functools.partial(_decode_kernel, mode=mode, scale=scale, vheads=vheads, pps=pps, lam_init=lam_init),
        grid_spec=pltpu.PrefetchScalarGridSpec(
            num_scalar_prefetch=1,
            grid=(bsz, npages // pps),
            in_specs=in_specs,
            out_specs=pl.BlockSpec((1, out_rows, LANES), lambda b, j, pt_ref: (b, 0, 0)),
            scratch_shapes=scratch),
        out_shape=jax.ShapeDtypeStruct((bsz, out_rows, LANES), F32),
        compiler_params=_params("parallel", "arbitrary"),
        name="decode_" + mode,
    )(pt, *args)


def _cumsum_kernel(x_ref, o_ref, carry_ref, *, tt):
    @pl.when(pl.program_id(1) == 0)
    def _():
        carry_ref[...] = jnp.zeros(carry_ref.shape, F32)

    x = x_ref[0]
    r = lax.broadcasted_iota(I32, (tt, tt), 0)
    c = lax.broadcasted_iota(I32, (tt, tt), 1)
    tri = jnp.where(c <= r, 1.0, 0.0).astype(BF16)
    hi, mid, lo = _split3(x)
    cs = (jnp.dot(tri, hi, preferred_element_type=F32) + jnp.dot(tri, mid, preferred_element_type=F32)
          + jnp.dot(tri, lo, preferred_element_type=F32)) + carry_ref[...]
    o_ref[0] = cs
    carry_ref[...] = cs[tt - 1:tt, :]


def cumsum_time(x, tt=256):
    bsz, s, h = x.shape
    tt = min(tt, s)
    return pl.pallas_call(
        functools.partial(_cumsum_kernel, tt=tt),
        grid=(bsz, s // tt),
        in_specs=[pl.BlockSpec((1, tt, h), lambda b, i: (b, i, 0))],
        out_specs=pl.BlockSpec((1, tt, h), lambda b, i: (b, i, 0)),
        out_shape=jax.ShapeDtypeStruct((bsz, s, h), F32),
        scratch_shapes=[pltpu.VMEM((1, h), F32)],
        compiler_params=_params("parallel", "arbitrary"),
        name="cumsum_time",
    )(x)


def _logsig_kernel(z_ref, b_ref, o_ref):
    o_ref[...] = jax.nn.log_sigmoid(z_ref[...] + b_ref[...])


def log_sigmoid_bias(z, bias, tm=1024):
    m, n = z.shape
    tm = min(tm, m)
    return pl.pallas_call(
        _logsig_kernel,
        grid=(m // tm,),
        in_specs=[pl.BlockSpec((tm, n), lambda i: (i, 0)), pl.BlockSpec((1, n), lambda i: (0, 0))],
        out_specs=pl.BlockSpec((tm, n), lambda i: (i, 0)),
        out_shape=jax.ShapeDtypeStruct((m, n), F32),
        compiler_params=_params("parallel"),
        name="log_sigmoid_bias",
    )(z, bias.reshape(1, n).astype(F32))


_NTOP = PEER_TOPK + 1
_STAIR = [(a, _NTOP // (a + 1)) for a in range(_NTOP)]
_STAIR_ROWS = ((sum(nb for _, nb in _STAIR) + 7) // 8) * 8


def _peer_score_kernel(q_ref, keys_ref, s1_ref, e1_ref, thr_ref, e2_ref, top_ref, cand_ref, cv_ref):
    q = q_ref[...]
    top_ref[...] = jnp.full(top_ref.shape, -jnp.inf, F32)
    cv_ref[...] = jnp.full(cv_ref.shape, -jnp.inf, F32)
    sc = []
    for c in range(2):
        qc = q[:, c * PEER_NKEYS:(c + 1) * PEER_NKEYS].astype(BF16)
        kc = keys_ref[c].astype(BF16)
        s = lax.dot_general(kc, qc, (((1,), (1,)), ((), ())), preferred_element_type=F32)
        sc.append(s)
        for k in range(_NTOP):
            mx = jnp.max(s, axis=0, keepdims=True)
            top_ref[c, k:k + 1, :] = mx
            s = jnp.where(s == mx, -jnp.inf, s)
    v1 = top_ref[0]
    v2 = top_ref[1]
    cand_ref[...] = jnp.full(cand_ref.shape, -jnp.inf, F32)
    off = 0
    for a, nb in _STAIR:
        cand_ref[off:off + nb, :] = v1[a:a + 1, :] + v2[0:nb, :]
        off += nb
    cand = cand_ref[...]
    for k in range(_NTOP):
        mx = jnp.max(cand, axis=0, keepdims=True)
        cv_ref[k:k + 1, :] = mx
        cand = jnp.where(cand == mx, -jnp.inf, cand)
    cv = cv_ref[...]
    tau = 0.5 * (cv[PEER_TOPK - 1:PEER_TOPK, :] + cv[PEER_TOPK:PEER_TOPK + 1, :])
    z = jnp.sum(jnp.exp(cv[0:PEER_TOPK, :] - cv[0:1, :]), axis=0, keepdims=True)
    s1, s2 = sc
    s1_ref[0] = s1
    e1_ref[0] = jnp.exp(s1 - v1[0:1, :])
    thr_ref[0] = tau - s2
    e2_ref[0] = jnp.exp(s2 - v2[0:1, :]) / z


def peer_scores(q, keys, tm=256):
    t = q.shape[0]
    tm = min(tm, t)
    assert t % tm == 0 and tm % LANES == 0
    shp = jax.ShapeDtypeStruct((PEER_HEADS, PEER_NKEYS, t), F32)
    ospec = pl.BlockSpec((1, PEER_NKEYS, tm), lambda i, h: (h, 0, i))
    return pl.pallas_call(
        _peer_score_kernel,
        grid=(t // tm, PEER_HEADS),
        in_specs=[pl.BlockSpec((tm, 2 * PEER_NKEYS), lambda i, h: (i, h)),
                  pl.BlockSpec(keys.shape, lambda i, h: (0, 0, 0))],
        out_specs=[ospec, ospec, ospec, ospec],
        out_shape=[shp, shp, shp, shp],
        scratch_shapes=[pltpu.VMEM((2, _NTOP + 7, tm), F32), pltpu.VMEM((_STAIR_ROWS, tm), F32),
                        pltpu.VMEM((_NTOP + 7, tm), F32)],
        compiler_params=_params("parallel", "parallel"),
        name="peer_scores",
    )(q, keys)


def _peer_dense_kernel(x_ref, g_ref, u_ref, vt_ref, s1_ref, e1_ref, thr_ref, e2_ref, o_ref,
                       xn_ref, ht_ref, at_ref, acc_ref, *, ci):
    c = pl.program_id(1)
    nc = pl.num_programs(1)
    tm = x_ref.shape[0]

    @pl.when(c == 0)
    def _():
        x = x_ref[...]
        xn_ref[...] = (x * lax.rsqrt(jnp.mean(x * x, axis=-1, keepdims=True) + EPS) * g_ref[...]).astype(BF16)
        acc_ref[...] = jnp.zeros(acc_ref.shape, F32)

    ht_ref[...] = lax.dot_general(u_ref[...], xn_ref[...], (((1,), (1,)), ((), ())),
                                  preferred_element_type=F32)

    def first_key(i, carry):
        row = c * ci + i
        r0 = pl.multiple_of(i * PEER_NKEYS, PEER_NKEYS)
        s1_rows = [s1_ref[h, pl.ds(row, 1), :] for h in range(PEER_HEADS)]
        e1_rows = [e1_ref[h, pl.ds(row, 1), :] for h in range(PEER_HEADS)]
        for t0 in range(0, tm, LANES):
            lanes = pl.ds(t0, LANES)
            gate = jnp.zeros((PEER_NKEYS, LANES), F32)
            for h in range(PEER_HEADS):
                s1 = s1_rows[h][:, t0:t0 + LANES]
                e1 = e1_rows[h][:, t0:t0 + LANES]
                gate = gate + jnp.where(s1 >= thr_ref[h, :, lanes], e2_ref[h, :, lanes], 0.0) * e1
            hh = ht_ref[pl.ds(r0, PEER_NKEYS), lanes]
            at_ref[pl.ds(r0, PEER_NKEYS), lanes] = (gate * jax.nn.gelu(hh)).astype(BF16)
        return carry

    lax.fori_loop(0, ci, first_key, 0)
    acc_ref[...] += jnp.dot(vt_ref[...], at_ref[...], preferred_element_type=F32)

    @pl.when(c == nc - 1)
    def _():
        o_ref[...] = x_ref[...] + acc_ref[...].T


def peer_dense(x, gain, u_bf, vt_bf, s1, e1, thr, e2, tm=512, ci=8):
    t, d = x.shape
    ne = u_bf.shape[0]
    tm = min(tm, t)
    ec = ci * PEER_NKEYS
    assert t % tm == 0 and ne % ec == 0
    once = pl.Buffered(1)
    sspec = pl.BlockSpec((PEER_HEADS, PEER_NKEYS, tm), lambda i, c: (0, 0, i), pipeline_mode=once)
    return pl.pallas_call(
        functools.partial(_peer_dense_kernel, ci=ci),
        grid=(t // tm, ne // ec),
        in_specs=[pl.BlockSpec((tm, d), lambda i, c: (i, 0), pipeline_mode=once),
                  pl.BlockSpec((1, d), lambda i, c: (0, 0)),
                  pl.BlockSpec((ec, d), lambda i, c: (c, 0)),
                  pl.BlockSpec((d, ec), lambda i, c: (0, c)),
                  sspec, sspec, sspec, sspec],
        out_specs=pl.BlockSpec((tm, d), lambda i, c: (i, 0)),
        out_shape=jax.ShapeDtypeStruct((t, d), F32),
        scratch_shapes=[pltpu.VMEM((tm, d), BF16), pltpu.VMEM((ec, tm), F32), pltpu.VMEM((ec, tm), BF16),
                        pltpu.VMEM((d, tm), F32)],
        compiler_params=_params("parallel", "arbitrary"),
        name="peer_dense",
    )(x, gain.reshape(1, d).astype(F32), u_bf, vt_bf, s1, e1, thr, e2)


def peer_layer(x, gain, wq_bf, keys, u_bf, vt_bf):
    t = x.shape[0]
    tp = -(-t // LANES) * LANES
    xp = jnp.pad(x, ((0, tp - t), (0, 0))) if tp != t else x
    q = rms_matmul(xp, gain, wq_bf)
    s1, e1, thr, e2 = peer_scores(q, keys)
    out = peer_dense(xp, gain, u_bf, vt_bf, s1, e1, thr, e2)
    return out[:t]


def kernel(x_prompt, x_sample, mem_prompt, cache_diff_k, cache_diff_v, cache_fox_k, cache_fox_v, cache_fox_logf, state_conv, cache_mem_k, cache_mem_v, page_table, norm_mix, norm_xattn, norm_ffn, even_w_in, even_w_out, diff_q_gain, diff_k_gain, diff_lambda, diff_out_gain, conv_w, conv_b, conv_ln_g, conv_ln_b, odd_w_in, fox_forget_bias, fox_q_gain, fox_k_gain, odd_w_out, mem_norm, xattn_wq, xattn_wk, xattn_wv, xattn_wo, xattn_q_gain, xattn_k_gain, peer_wq, peer_keys, peer_u, peer_v):
    bp, seq, d = x_prompt.shape
    bs = x_sample.shape[0]
    depth = norm_mix.shape[0]
    half = d // 2
    n_pages = page_table.shape[1]
    past = n_pages * PAGE
    n_phys = cache_diff_k.shape[1]
    mem_len = mem_prompt.shape[1]
    xw = xattn_wq.shape[2]
    tp = bp * seq
    srow = 8

    xp = x_prompt.reshape(tp, d)
    xs = x_sample.reshape(bs, d)
    memf = mem_prompt.reshape(bp * mem_len, d)

    n_even, n_odd = cache_diff_k.shape[0], cache_fox_k.shape[0]
    dk_t = jnp.transpose(cache_diff_k, (0, 1, 3, 4, 5, 2)).reshape(n_even * n_phys, DEC_ROWS, DIFF_DK, PAGE)
    dv_c = cache_diff_v.reshape(n_even * n_phys, PAGE * DIFF_HEADS, LANES)
    fk_c = cache_fox_k.reshape(n_odd * n_phys, PAGE * FOX_HEADS, LANES)
    fv_c = cache_fox_v.reshape(n_odd * n_phys, PAGE * FOX_HEADS, LANES)
    lf_t = jnp.swapaxes(cache_fox_logf, 2, 3).reshape(n_odd * n_phys, FOX_HEADS, PAGE)

    rope_p = rope_tables(jnp.arange(seq, dtype=I32), seq)
    rope_s = rope_tables(jnp.full((1,), past, I32), bs)

    outs = {k: [] for k in ("dk_p", "dv_p", "dk_s", "dv_s", "fk_p", "fv_p", "fl_p", "fk_s", "fv_s", "fl_s",
                            "cv_p", "cv_s", "mk_p", "mv_p")}

    for l in range(depth):
        i = l // 2
        if l % 2 == 0:
            lam_init = 0.8 - 0.6 * math.exp(-0.3 * l)
            w_in = even_w_in[i].astype(BF16)
            w_out = even_w_out[i].astype(BF16)
            nq = half // LANES
            res = []
            for x, rope in ((xp, rope_p), (xs, rope_s)):
                z = rms_matmul(x, norm_mix[l], w_in)
                qn = head_norm(z, 0, nq, diff_q_gain[i], DIFF_DK, rope)
                kn = head_norm(z, nq, nq, diff_k_gain[i], DIFF_DK, rope)
                v = z[:, 2 * half:3 * half]
                u = glu(z, 3 * half, 4 * half, half)
                res.append((z, qn, kn, v, u))
            z, qn, kn, v, u = res[0]
            attn = flash_attention(qn, 0, kn, 0, z, 2 * nq, bsz=bp, heads=DIFF_HEADS, sq=seq, sk=seq,
                                   mode="diff", causal=True, scale=DIFF_DK ** -0.5, tq=512, tk=512,
                                   lam_params=diff_lambda[i], lam_init=lam_init)
            attn = head_norm(attn, 0, nq, diff_out_gain[i], LANES, out_scale=1.0 - lam_init)
            u3 = u.reshape(bp, seq, half)
            conv = conv_branch(u3, jnp.zeros((bp, CONV_HALO, half), F32), conv_w[i], conv_b[i],
                               conv_ln_g[i], conv_ln_b[i]).reshape(tp, half)
            xp = rms_matmul(jnp.concatenate([attn, conv], axis=1), None, w_out, res=xp)
            outs["dk_p"].append(kn.reshape(bp, seq, DIFF_HEADS, 2, DIFF_DK))
            outs["dv_p"].append(v.reshape(bp, seq, DIFF_HEADS, 2 * DIFF_DK))
            outs["cv_p"].append(u3[:, seq - (CONV_WIDTH - 1):])
            z, qn, kn, v, u = res[1]
            attn = decode_attention(qn.reshape(bs, DEC_ROWS, DIFF_DK), kn.reshape(bs, DEC_ROWS, DIFF_DK),
                                    v.reshape(bs, DIFF_HEADS, LANES), dk_t, dv_c, page_table, i * n_phys,
                                    mode="diff", scale=DIFF_DK ** -0.5,
                                    lam_params=diff_lambda[i], lam_init=lam_init).reshape(bs, half)
            attn = head_norm(attn, 0, nq, diff_out_gain[i], LANES, out_scale=1.0 - lam_init)
            st = state_conv[i]
            init = jnp.concatenate([jnp.zeros((bs, CONV_HALO - (CONV_WIDTH - 1), half), F32), st], axis=1)
            u_pad = jnp.concatenate([u[:, None, :], jnp.zeros((bs, srow - 1, half), F32)], axis=1)
            conv = conv_branch(u_pad, init, conv_w[i], conv_b[i], conv_ln_g[i], conv_ln_b[i])[:, 0]
            xs = rms_matmul(jnp.concatenate([attn, conv], axis=1), None, w_out, res=xs)
            outs["dk_s"].append(kn.reshape(bs, 1, DIFF_HEADS, 2, DIFF_DK))
            outs["dv_s"].append(v.reshape(bs, 1, DIFF_HEADS, 2 * DIFF_DK))
            outs["cv_s"].append(jnp.concatenate([st[:, 1:], u[:, None, :]], axis=1))
        else:
            w_in = odd_w_in[i]
            w_main = w_in[:, :3 * d].astype(BF16)
            w_gate = jnp.pad(w_in[:, 3 * d:], ((0, 0), (0, LANES - FOX_HEADS))).astype(BF16)
            w_out = odd_w_out[i].astype(BF16)
            nh = d // LANES
            fbias = jnp.pad(fox_forget_bias[i], (0, LANES - FOX_HEADS))
            res = []
            for x in (xp, xs):
                z = rms_matmul(x, norm_mix[l], w_main)
                zg = rms_matmul(x, norm_mix[l], w_gate)
                qn = head_norm(z, 0, nh, fox_q_gain[i], LANES)
                kn = head_norm(z, nh, nh, fox_k_gain[i], LANES)
                v = z[:, 2 * d:3 * d]
                logf = log_sigmoid_bias(zg, fbias)[:, :FOX_HEADS]
                res.append((z, qn, kn, v, logf))
            z, qn, kn, v, logf = res[0]
            cum = cumsum_time(logf.reshape(bp, seq, FOX_HEADS))
            cum_t = jnp.swapaxes(cum, 1, 2).reshape(bp * FOX_HEADS, 1, seq)
            attn = flash_attention(qn, 0, kn, 0, z, 2 * nh, bsz=bp, heads=FOX_HEADS, sq=seq, sk=seq,
                                   mode="fox", causal=True, scale=LANES ** -0.5, tq=1024, tk=512, ck=cum_t)
            xp = rms_matmul(attn, None, w_out, res=xp)
            outs["fk_p"].append(kn.reshape(bp, seq, FOX_HEADS, LANES))
            outs["fv_p"].append(v.reshape(bp, seq, FOX_HEADS, LANES))
            outs["fl_p"].append(logf.reshape(bp, seq, FOX_HEADS))
            z, qn, kn, v, logf = res[1]
            h3 = (bs, FOX_HEADS, LANES)
            attn = decode_attention(qn.reshape(h3), kn.reshape(h3), v.reshape(h3), fk_c, fv_c, page_table,
                                    i * n_phys, mode="fox", scale=LANES ** -0.5,
                                    lf_cache_t=lf_t, lf_new=logf).reshape(bs, d)
            xs = rms_matmul(attn, None, w_out, res=xs)
            outs["fk_s"].append(kn.reshape(bs, 1, FOX_HEADS, LANES))
            outs["fv_s"].append(v.reshape(bs, 1, FOX_HEADS, LANES))
            outs["fl_s"].append(logf.reshape(bs, 1, FOX_HEADS))

        wq = xattn_wq[l].astype(BF16)
        wo = xattn_wo[l].astype(BF16)
        wkv = jnp.concatenate([xattn_wk[l], xattn_wv[l]], axis=1).astype(BF16)
        nxh = xw // LANES
        kv = rms_matmul(memf, mem_norm[l], wkv)
        mk = head_norm(kv, 0, nxh, xattn_k_gain[l], LANES)
        mv = kv[:, xw:]
        outs["mk_p"].append(mk.reshape(bp, mem_len, X_HEADS, LANES))
        outs["mv_p"].append(mv.reshape(bp, mem_len, X_HEADS, LANES))
        xq = head_norm(rms_matmul(xp, norm_xattn[l], wq), 0, nxh, xattn_q_gain[l], LANES)
        o = flash_attention(xq, 0, mk, 0, kv, nxh, bsz=bp, heads=X_HEADS, sq=seq, sk=mem_len,
                            mode="plain", causal=False, scale=LANES ** -0.5, tq=1024, tk=mem_len)
        xp = rms_matmul(o, None, wo, res=xp)
        xq = head_norm(rms_matmul(xs, norm_xattn[l], wq), 0, nxh, xattn_q_gain[l], LANES)
        xq = jnp.pad(xq[:, None, :], ((0, 0), (0, srow - 1), (0, 0))).reshape(bs * srow, xw)
        o = flash_attention(xq, 0, cache_mem_k[l].reshape(bs * mem_len, xw), 0,
                            cache_mem_v[l].reshape(bs * mem_len, xw), 0, bsz=bs, heads=X_HEADS,
                            sq=srow, sk=mem_len, mode="plain", causal=False, scale=LANES ** -0.5,
                            tq=srow, tk=mem_len)
        o = o.reshape(bs, srow, xw)[:, 0]
        xs = rms_matmul(o, None, wo, res=xs)

        pwq = peer_wq[l].astype(BF16)
        u_bf = peer_u[l].astype(BF16)
        vt_bf = peer_v[l].astype(BF16).T
        xp = peer_layer(xp, norm_ffn[l], pwq, peer_keys[l], u_bf, vt_bf)
        xs = peer_layer(xs, norm_ffn[l], pwq, peer_keys[l], u_bf, vt_bf)

    st = lambda k: jnp.stack(outs[k])
    return (xp.reshape(bp, seq, d), xs.reshape(bs, 1, d),
            st("dk_p"), st("dv_p"), st("fk_p"), st("fv_p"), st("fl_p"),
            st("cv_p"), st("mk_p"), st("mv_p"),
            st("dk_s"), st("dv_s"), st("fk_s"), st("fv_s"), st("fl_s"), st("cv_s"))
```

```python
import functools
import math

import jax
import jax.numpy as jnp
from jax import lax
from jax.experimental import pallas as pl
from jax.experimental.pallas import tpu as pltpu

F32 = jnp.float32
BF16 = jnp.bfloat16
I32 = jnp.int32

EPS = 1e-6
NEG = -1e30
ROPE_THETA = 500000.0
LANES = 128
MXU_WIDTH = 256
V7X_VMEM_BYTES = 64 * 1024 * 1024
VMEM_LIMIT = V7X_VMEM_BYTES - 8 * 1024 * 1024

DIFF_HEADS = 8
DIFF_DK = 64
DIFF_ROT = 16
FOX_HEADS = 16
X_HEADS = 4
PEER_HEADS = 8
PEER_NKEYS = 128
PEER_TOPK = 16
CONV_WIDTH = 31
CONV_HALO = 32
PAGE = 128


def _params(*sem):
    return pltpu.CompilerParams(dimension_semantics=sem, vmem_limit_bytes=VMEM_LIMIT)


def _rms_matmul_kernel(*refs, use_norm, has_res):
    if has_res:
        x_ref, g_ref, w_ref, r_ref, o_ref, xn_ref = refs
    else:
        x_ref, g_ref, w_ref, o_ref, xn_ref = refs

    @pl.when(pl.program_id(1) == 0)
    def _():
        x = x_ref[...]
        if use_norm:
            x = x * lax.rsqrt(jnp.mean(x * x, axis=-1, keepdims=True) + EPS) * g_ref[...]
        xn_ref[...] = x.astype(BF16)

    acc = jnp.dot(xn_ref[...], w_ref[...], preferred_element_type=F32)
    if has_res:
        acc = r_ref[...] + acc
    o_ref[...] = acc


def rms_matmul(x, gain, w, res=None, tm=1024, tn=512):
    m, k = x.shape
    n = w.shape[1]
    tm = min(tm, m)
    tn = min(tn, n)
    assert m % tm == 0 and n % tn == 0, (m, n, tm, tn)
    use_norm = gain is not None
    g = (gain if use_norm else jnp.ones((k,), F32)).reshape(1, k).astype(F32)
    in_specs = [pl.BlockSpec((tm, k), lambda i, j: (i, 0)),
                pl.BlockSpec((1, k), lambda i, j: (0, 0)),
                pl.BlockSpec((k, tn), lambda i, j: (0, j))]
    args = [x, g, w]
    if res is not None:
        in_specs.append(pl.BlockSpec((tm, tn), lambda i, j: (i, j)))
        args.append(res)
    return pl.pallas_call(
        functools.partial(_rms_matmul_kernel, use_norm=use_norm, has_res=res is not None),
        grid=(m // tm, n // tn),
        in_specs=in_specs,
        out_specs=pl.BlockSpec((tm, tn), lambda i, j: (i, j)),
        out_shape=jax.ShapeDtypeStruct((m, n), F32),
        scratch_shapes=[pltpu.VMEM((tm, k), BF16)],
        compiler_params=_params("parallel", "arbitrary"),
        name="rms_matmul",
    )(*args)


def _head_norm_kernel(*refs, gw, rope, out_scale):
    if rope:
        a_ref, g_ref, cos_ref, sin_ref, o_ref = refs
    else:
        a_ref, g_ref, o_ref = refs
    x = a_ref[...]
    xx = x * x
    lane = lax.broadcasted_iota(I32, x.shape, 1)
    if gw == LANES:
        ms = jnp.mean(xx, axis=-1, keepdims=True)
    else:
        lo = lane < gw
        s_lo = jnp.sum(jnp.where(lo, xx, 0.0), axis=-1, keepdims=True)
        s_hi = jnp.sum(jnp.where(lo, 0.0, xx), axis=-1, keepdims=True)
        ms = jnp.where(lo, s_lo, s_hi) * (1.0 / gw)
    y = x * lax.rsqrt(ms + EPS) * g_ref[...]
    if rope:
        half = DIFF_ROT // 2
        first = (lane & (gw - 1)) < half
        partner = jnp.where(first, pltpu.roll(y, LANES - half, 1), pltpu.roll(y, half, 1))
        y = y * cos_ref[...] + partner * sin_ref[...]
    if out_scale != 1.0:
        y = y * out_scale
    o_ref[...] = y


def head_norm(a, col0, ncols, gain, gw, rope_tabs=None, out_scale=1.0, tm=1024):
    m = a.shape[0]
    tm = min(tm, m)
    if rope_tabs is not None:
        tm = min(tm, rope_tabs[0].shape[0])
    assert m % tm == 0
    g = jnp.tile(gain.astype(F32), LANES // gw).reshape(1, LANES)
    in_specs = [pl.BlockSpec((tm, LANES), lambda i, j: (i, col0 + j)),
                pl.BlockSpec((1, LANES), lambda i, j: (0, 0))]
    args = [a, g]
    if rope_tabs is not None:
        cos_t, sin_t = rope_tabs
        nrep = cos_t.shape[0] // tm
        in_specs += [pl.BlockSpec((tm, LANES), lambda i, j: (i % nrep, 0)),
                     pl.BlockSpec((tm, LANES), lambda i, j: (i % nrep, 0))]
        args += [cos_t, sin_t]
    return pl.pallas_call(
        functools.partial(_head_norm_kernel, gw=gw, rope=rope_tabs is not None, out_scale=out_scale),
        grid=(m // tm, ncols),
        in_specs=in_specs,
        out_specs=pl.BlockSpec((tm, LANES), lambda i, j: (i, j)),
        out_shape=jax.ShapeDtypeStruct((m, ncols * LANES), F32),
        compiler_params=_params("parallel", "parallel"),
        name="head_norm",
    )(*args)


def rope_tables(pos, rows):
    half = DIFF_ROT // 2
    inv = jnp.power(ROPE_THETA, -jnp.arange(half, dtype=F32) * 2.0 / DIFF_ROT)
    ang = pos.astype(F32)[:, None] * inv[None, :]
    cos, sin = jnp.cos(ang), jnp.sin(ang)
    ones = jnp.ones((pos.shape[0], DIFF_DK - DIFF_ROT), F32)
    cos64 = jnp.concatenate([cos, cos, ones], axis=1)
    sin64 = jnp.concatenate([-sin, sin, 0.0 * ones], axis=1)
    cos_t = jnp.tile(cos64, (1, LANES // DIFF_DK))
    sin_t = jnp.tile(sin64, (1, LANES // DIFF_DK))
    if cos_t.shape[0] != rows:
        cos_t = jnp.broadcast_to(cos_t[:1], (rows, LANES))
        sin_t = jnp.broadcast_to(sin_t[:1], (rows, LANES))
    return cos_t, sin_t


def _glu_kernel(a_ref, b_ref, o_ref):
    o_ref[...] = a_ref[...] * jax.nn.sigmoid(b_ref[...])


def glu(z, cola, colb, width, tm=1024, tn=512):
    m = z.shape[0]
    tm = min(tm, m)
    nb = width // tn
    return pl.pallas_call(
        _glu_kernel,
        grid=(m // tm, nb),
        in_specs=[pl.BlockSpec((tm, tn), lambda i, j: (i, cola // tn + j)),
                  pl.BlockSpec((tm, tn), lambda i, j: (i, colb // tn + j))],
        out_specs=pl.BlockSpec((tm, tn), lambda i, j: (i, j)),
        out_shape=jax.ShapeDtypeStruct((m, width), F32),
        compiler_params=_params("parallel", "parallel"),
        name="glu",
    )(z, z)


def _conv_kernel(cur_ref, prev_ref, init_ref, w_ref, b_ref, g_ref, beta_ref, o_ref, buf_ref, *, tt, rc):
    i = pl.program_id(1)

    @pl.when(i == 0)
    def _():
        buf_ref[0:CONV_HALO, :] = init_ref[0]

    if tt >= CONV_HALO:
        @pl.when(i > 0)
        def _():
            buf_ref[0:CONV_HALO, :] = prev_ref[0, tt - CONV_HALO:tt, :]

    buf_ref[CONV_HALO:CONV_HALO + tt, :] = cur_ref[0]
    off = CONV_HALO - (CONV_WIDTH - 1)
    for r0 in range(0, tt, rc):
        acc = jnp.zeros((rc, cur_ref.shape[2]), F32)
        for j in range(CONV_WIDTH):
            acc = acc + w_ref[j:j + 1, :] * buf_ref[r0 + off + j:r0 + off + j + rc, :]
        y = acc + b_ref[...]
        mu = jnp.mean(y, axis=-1, keepdims=True)
        yc = y - mu
        var = jnp.mean(yc * yc, axis=-1, keepdims=True)
        y = yc * lax.rsqrt(var + EPS) * g_ref[...] + beta_ref[...]
        o_ref[0, r0:r0 + rc, :] = y * jax.nn.sigmoid(y)


def conv_branch(u, init, w, b, ln_g, ln_b, tt=128):
    bsz, length, ch = u.shape
    tt = min(tt, length)
    assert length % tt == 0 and (tt >= CONV_HALO or length == tt)
    rc = min(16, tt)
    wpad = jnp.concatenate([w, jnp.zeros((CONV_HALO - CONV_WIDTH, ch), F32)], axis=0)
    row = lambda v: v.reshape(1, ch).astype(F32)
    return pl.pallas_call(
        functools.partial(_conv_kernel, tt=tt, rc=rc),
        grid=(bsz, length // tt),
        in_specs=[pl.BlockSpec((1, tt, ch), lambda bi, i: (bi, i, 0)),
                  pl.BlockSpec((1, tt, ch), lambda bi, i: (bi, jnp.maximum(i - 1, 0), 0)),
                  pl.BlockSpec((1, CONV_HALO, ch), lambda bi, i: (bi, 0, 0)),
                  pl.BlockSpec((CONV_HALO, ch), lambda bi, i: (0, 0)),
                  pl.BlockSpec((1, ch), lambda bi, i: (0, 0)),
                  pl.BlockSpec((1, ch), lambda bi, i: (0, 0)),
                  pl.BlockSpec((1, ch), lambda bi, i: (0, 0))],
        out_specs=pl.BlockSpec((1, tt, ch), lambda bi, i: (bi, i, 0)),
        out_shape=jax.ShapeDtypeStruct((bsz, length, ch), F32),
        scratch_shapes=[pltpu.VMEM((CONV_HALO + tt, ch), F32)],
        compiler_params=_params("parallel", "arbitrary"),
        name="conv_branch",
    )(u, u, init, wpad, row(b), row(ln_g), row(ln_b))


LOG2E = 1.4426950408889634


def _flash_kernel(*refs, mode, causal, qscale, tq, tk, lam_init):
    qi_ref, ki_ref, last_ref = refs[:3]
    if mode == "fox":
        q_ref, k_ref, v_ref, ck_ref, o_ref, qs_ref, m_ref, l_ref, acc_ref = refs[3:]
    elif mode == "diff":
        q_ref, k_ref, v_ref, lam_ref, o_ref, qs_ref, m_ref, l_ref, acc_ref = refs[3:]
    else:
        q_ref, k_ref, v_ref, o_ref, qs_ref, m_ref, l_ref, acc_ref = refs[3:]
    step = pl.program_id(2)
    qi = qi_ref[step]
    ki = ki_ref[step]

    @pl.when(ki == 0)
    def _():
        q = q_ref[...] * qscale
        if mode == "diff":
            lane = lax.broadcasted_iota(I32, q.shape, 1)
            lo = lane < DIFF_DK
            qs_ref[0:tq, :] = jnp.where(lo, q, 0.0).astype(BF16)
            qs_ref[tq:2 * tq, :] = jnp.where(lo, 0.0, q).astype(BF16)
        else:
            qs_ref[...] = q.astype(BF16)
        m_ref[...] = jnp.full(m_ref.shape, NEG, F32)
        l_ref[...] = jnp.zeros(l_ref.shape, F32)
        acc_ref[...] = jnp.zeros(acc_ref.shape, F32)

    def compute(masked):
        s = lax.dot_general(qs_ref[...], k_ref[...].astype(BF16), (((1,), (1,)), ((), ())),
                            preferred_element_type=F32)
        if mode == "fox":
            s = s - ck_ref[0] * LOG2E
        if masked:
            r = lax.broadcasted_iota(I32, s.shape, 0)
            if mode == "diff":
                r = jnp.where(r >= tq, r - tq, r)
            c = lax.broadcasted_iota(I32, s.shape, 1)
            s = jnp.where(ki * tk + c <= qi * tq + r, s, NEG)
        m_prev = m_ref[...]
        m_next = jnp.maximum(m_prev, jnp.max(s, axis=1, keepdims=True))
        alpha = jnp.exp2(m_prev - m_next)
        p = jnp.exp2(s - pltpu.repeat(m_next, tk // LANES, 1))
        l_ref[...] = alpha * l_ref[...] + jnp.sum(p, axis=1, keepdims=True)
        acc_ref[...] = alpha * acc_ref[...] + jnp.dot(p.astype(BF16), v_ref[...].astype(BF16),
                                                      preferred_element_type=F32)
        m_ref[...] = m_next

    if causal:
        crosses = ki * tk + tk - 1 > qi * tq
        pl.when(crosses)(functools.partial(compute, True))
        pl.when(jnp.logical_not(crosses))(functools.partial(compute, False))
    else:
        compute(False)

    @pl.when(last_ref[step] == 1)
    def _():
        o = acc_ref[...] / l_ref[...]
        if mode == "diff":
            lp = lam_ref[...]
            lam = (jnp.exp(jnp.sum(lp[0:1] * lp[1:2], axis=-1, keepdims=True))
                   - jnp.exp(jnp.sum(lp[2:3] * lp[3:4], axis=-1, keepdims=True)) + lam_init)
            o = o[0:tq] - lam * o[tq:2 * tq]
        o_ref[...] = o


def flash_attention(qa, qc0, ka, kc0, va, vc0, *, bsz, heads, sq, sk, mode, causal, scale, tq, tk,
                    ck=None, lam_params=None, lam_init=0.0):
    tq = min(tq, sq)
    tk = min(tk, sk)
    nq, nk = sq // tq, sk // tk
    assert sq % tq == 0 and sk % tk == 0 and tk % LANES == 0
    rows = 2 * tq if mode == "diff" else tq
    pairs = []
    for qi in range(nq):
        kmax = (qi * tq + tq - 1) // tk if causal else nk - 1
        pairs += [(qi, ki, int(ki == kmax)) for ki in range(kmax + 1)]
    qi_l, ki_l, last_l = (jnp.asarray(v, I32) for v in zip(*pairs))

    in_specs = [pl.BlockSpec((tq, LANES), lambda b, h, p, qr, kr, lr: (b * nq + qr[p], qc0 + h)),
                pl.BlockSpec((tk, LANES), lambda b, h, p, qr, kr, lr: (b * nk + kr[p], kc0 + h)),
                pl.BlockSpec((tk, LANES), lambda b, h, p, qr, kr, lr: (b * nk + kr[p], vc0 + h))]
    args = [qa, ka, va]
    scratch = [pltpu.VMEM((rows, LANES), BF16), pltpu.VMEM((rows, LANES), F32),
               pltpu.VMEM((rows, LANES), F32), pltpu.VMEM((rows, LANES), F32)]
    if mode == "fox":
        in_specs.append(pl.BlockSpec((1, 1, tk), lambda b, h, p, qr, kr, lr: (b * heads + h, 0, kr[p])))
        args.append(ck)
    elif mode == "diff":
        in_specs.append(pl.BlockSpec(lam_params.shape, lambda b, h, p, qr, kr, lr: (0, 0)))
        args.append(lam_params)
    return pl.pallas_call(
        functools.partial(_flash_kernel, mode=mode, causal=causal, qscale=scale * LOG2E, tq=tq, tk=tk,
                          lam_init=lam_init),
        grid_spec=pltpu.PrefetchScalarGridSpec(
            num_scalar_prefetch=3,
            grid=(bsz, heads, len(pairs)),
            in_specs=in_specs,
            out_specs=pl.BlockSpec((tq, LANES), lambda b, h, p, qr, kr, lr: (b * nq + qr[p], h)),
            scratch_shapes=scratch),
        out_shape=jax.ShapeDtypeStruct((bsz * sq, heads * LANES), F32),
        compiler_params=_params("parallel", "parallel", "arbitrary"),
        name="flash_" + mode,
    )(qi_l, ki_l, last_l, *args)


DEC_ROWS = 16


def _split3(x):
    hi = x.astype(BF16)
    r1 = x - hi.astype(F32)
    mid = r1.astype(BF16)
    lo = (r1 - mid.astype(F32)).astype(BF16)
    return hi, mid, lo


def _decode_kernel(*refs, mode, scale, vheads, pps, lam_init):
    refs = refs[1:]
    if mode == "fox":
        q_ref, kn_ref, vn_ref, te_ref = refs[:4]
        kc_refs, vc_refs, lf_refs = refs[4:4 + pps], refs[4 + pps:4 + 2 * pps], refs[4 + 2 * pps:4 + 3 * pps]
        lfn_ref, o_ref, qb_ref, m_ref, l_ref, acc_ref, carry_ref = refs[4 + 3 * pps:]
    else:
        q_ref, kn_ref, qrep_ref, vn_ref = refs[:4]
        kc_refs, vc_refs = refs[4:4 + pps], refs[4 + pps:4 + 2 * pps]
        lam_ref, o_ref, m_ref, l_ref, acc_ref = refs[4 + 2 * pps:]
    j = pl.program_id(1)
    nsteps = pl.num_programs(1)
    rid = lax.broadcasted_iota(I32, (DEC_ROWS, LANES), 0)

    def lanes_x(x, k):
        return jnp.concatenate([x] * k, axis=1) if k > 1 else x

    @pl.when(j == 0)
    def _():
        q = q_ref[0]
        s_new = jnp.sum(q * kn_ref[0], axis=-1, keepdims=True) * scale
        m_ref[...] = jnp.broadcast_to(s_new, m_ref.shape)
        l_ref[...] = jnp.ones(l_ref.shape, F32)
        acc_ref[...] = vn_ref[0]
        if mode == "fox":
            qb_ref[...] = q.astype(BF16)
            carry_ref[...] = jnp.broadcast_to(lfn_ref[0], carry_ref.shape)

    for g in range(pps):
        if mode == "fox":
            wide = PAGE * DEC_ROWS // LANES
            s = lax.dot_general(qb_ref[...], kc_refs[g][0].astype(BF16), (((1,), (1,)), ((), ())),
                                preferred_element_type=F32) * scale
            lf = lf_refs[g][0]
            hi, mid, lo = _split3(lf)
            b3 = jnp.dot(jnp.concatenate([hi, mid, lo], axis=0), te_ref[...], preferred_element_type=F32)
            suffix = b3[0:DEC_ROWS] + b3[DEC_ROWS:2 * DEC_ROWS] + b3[2 * DEC_ROWS:3 * DEC_ROWS]
            col = lax.broadcasted_iota(I32, s.shape, 1)
            row = lax.broadcasted_iota(I32, s.shape, 0)
            s = jnp.where((col & (DEC_ROWS - 1)) == row, s + (suffix + lanes_x(carry_ref[...], wide)), NEG)
            carry_ref[...] = carry_ref[...] + jnp.sum(lf, axis=-1, keepdims=True)
        else:
            wide = 1
            s = jnp.sum(kc_refs[g][0] * qrep_ref[0], axis=1) * scale
        m_prev = m_ref[...]
        m_next = jnp.maximum(m_prev, jnp.max(s, axis=1, keepdims=True))
        alpha = jnp.exp(m_prev - m_next)
        p = jnp.exp(s - lanes_x(m_next, wide))
        l_ref[...] = alpha * l_ref[...] + jnp.sum(p, axis=1, keepdims=True)
        pb = p.astype(BF16)
        if mode == "fox":
            pv = jnp.dot(pb, vc_refs[g][0].astype(BF16), preferred_element_type=F32)
        else:
            pv = jnp.zeros((DEC_ROWS, LANES), F32)
            for h in range(vheads):
                vh = vc_refs[g][0, pl.ds(h, PAGE, stride=vheads), :].astype(BF16)
                pv = jnp.where((rid >> 1) == h, jnp.dot(pb, vh, preferred_element_type=F32), pv)
        acc_ref[...] = alpha * acc_ref[...] + pv
        m_ref[...] = m_next

    @pl.when(j == nsteps - 1)
    def _():
        inv = 1.0 / l_ref[...]
        if mode == "fox":
            o_ref[0] = acc_ref[...] * inv
        else:
            lp = lam_ref[...]
            lam = (jnp.exp(jnp.sum(lp[0:1] * lp[1:2], axis=-1, keepdims=True))
                   - jnp.exp(jnp.sum(lp[2:3] * lp[3:4], axis=-1, keepdims=True)) + lam_init)
            acc_ref[...] = acc_ref[...] * jnp.where((rid & 1) == 0, inv, -lam * inv)
            o_ref[0] = (acc_ref[pl.ds(0, DEC_ROWS // 2, stride=2), :]
                        + acc_ref[pl.ds(1, DEC_ROWS // 2, stride=2), :])


def decode_attention(q, k_new, v_new, k_cache, v_cache, page_table, page_base, *, mode, scale,
                     lf_cache_t=None, lf_new=None, lam_params=None, lam_init=0.0, pps=2):
    bsz = q.shape[0]
    npages = page_table.shape[1]
    assert npages % pps == 0
    pt = page_table.reshape(-1).astype(I32)
    vheads = v_cache.shape[1] // PAGE

    def page(g, ndim):
        return lambda b, j, pt_ref: ((page_base + pt_ref[b * npages + (npages - 1 - (j * pps + g))],)
                                     + (0,) * (ndim - 1))

    per_b = lambda shape: pl.BlockSpec((1,) + shape, lambda b, j, pt_ref: (b,) + (0,) * len(shape))
    kspecs = [pl.BlockSpec((1,) + k_cache.shape[1:], page(g, k_cache.ndim)) for g in range(pps)]
    vspecs = [pl.BlockSpec((1,) + v_cache.shape[1:], page(g, v_cache.ndim)) for g in range(pps)]
    if mode == "fox":
        te = (jnp.arange(PAGE, dtype=I32)[:, None] > (jnp.arange(PAGE * DEC_ROWS, dtype=I32)[None, :] // DEC_ROWS))
        te = te.astype(BF16)
        in_specs = ([per_b(q.shape[1:]), per_b(k_new.shape[1:]), per_b(v_new.shape[1:]),
                     pl.BlockSpec(te.shape, lambda b, j, pt_ref: (0, 0))] + kspecs + vspecs
                    + [pl.BlockSpec((1, DEC_ROWS, PAGE), page(g, 3)) for g in range(pps)]
                    + [per_b((DEC_ROWS, 1))])
        args = ([q, k_new, v_new, te] + [k_cache] * pps + [v_cache] * pps + [lf_cache_t] * pps
                + [lf_new.reshape(bsz, DEC_ROWS, 1)])
        scratch = [pltpu.VMEM((DEC_ROWS, LANES), BF16)]
        out_rows = DEC_ROWS
    else:
        qrep = jnp.broadcast_to(q[..., None], q.shape + (PAGE,))
        v_rows = jnp.repeat(v_new, 2, axis=1)
        in_specs = ([per_b(q.shape[1:]), per_b(k_new.shape[1:]), per_b(qrep.shape[1:]), per_b(v_rows.shape[1:])]
                    + kspecs + vspecs + [pl.BlockSpec(lam_params.shape, lambda b, j, pt_ref: (0, 0))])
        args = [q, k_new, qrep, v_rows] + [k_cache] * pps + [v_cache] * pps + [lam_params]
        scratch = []
        out_rows = DEC_ROWS // 2
    scratch += [pltpu.VMEM((DEC_ROWS, LANES), F32), pltpu.VMEM((DEC_ROWS, LANES), F32),
                pltpu.VMEM((DEC_ROWS, LANES), F32)]
    if mode == "fox":
        scratch.append(pltpu.VMEM((DEC_ROWS, LANES), F32))
    return pl.pallas_call(
        functools.partial(_decode_kernel, mode=mode, scale=scale, vheads=vheads, pps=pps, lam_init=lam_init),
        grid_spec=pltpu.PrefetchScalarGridSpec(
            num_scalar_prefetch=1,
            grid=(bsz, npages // pps),
            in_specs=in_specs,
            out_specs=pl.BlockSpec((1, out_rows, LANES), lambda b, j, pt_ref: (b, 0, 0)),
            scratch_shapes=scratch),
        out_shape=jax.ShapeDtypeStruct((bsz, out_rows, LANES), F32),
        compiler_params=_params("parallel", "arbitrary"),
        name="decode_" + mode,
    )(pt, *args)


def _cumsum_kernel(x_ref, o_ref, carry_ref, *, tt):
    @pl.when(pl.program_id(1) == 0)
    def _():
        carry_ref[...] = jnp.zeros(carry_ref.shape, F32)

    x = x_ref[0]
    r = lax.broadcasted_iota(I32, (tt, tt), 0)
    c = lax.broadcasted_iota(I32, (tt, tt), 1)
    tri = jnp.where(c <= r, 1.0, 0.0).astype(BF16)
    hi, mid, lo = _split3(x)
    cs = (jnp.dot(tri, hi, preferred_element_type=F32) + jnp.dot(tri, mid, preferred_element_type=F32)
          + jnp.dot(tri, lo, preferred_element_type=F32)) + carry_ref[...]
    o_ref[0] = cs
    carry_ref[...] = cs[tt - 1:tt, :]


def cumsum_time(x, tt=256):
    bsz, s, h = x.shape
    tt = min(tt, s)
    return pl.pallas_call(
        functools.partial(_cumsum_kernel, tt=tt),
        grid=(bsz, s // tt),
        in_specs=[pl.BlockSpec((1, tt, h), lambda b, i: (b, i, 0))],
        out_specs=pl.BlockSpec((1, tt, h), lambda b, i: (b, i, 0)),
        out_shape=jax.ShapeDtypeStruct((bsz, s, h), F32),
        scratch_shapes=[pltpu.VMEM((1, h), F32)],
        compiler_params=_params("parallel", "arbitrary"),
        name="cumsum_time",
    )(x)


def _logsig_kernel(z_ref, b_ref, o_ref):
    o_ref[...] = jax.nn.log_sigmoid(z_ref[...] + b_ref[...])


def log_sigmoid_bias(z, bias, tm=1024):
    m, n = z.shape
    tm = min(tm, m)
    return pl.pallas_call(
        _logsig_kernel,
        grid=(m // tm,),
        in_specs=[pl.BlockSpec((tm, n), lambda i: (i, 0)), pl.BlockSpec((1, n), lambda i: (0, 0))],
        out_specs=pl.BlockSpec((tm, n), lambda i: (i, 0)),
        out_shape=jax.ShapeDtypeStruct((m, n), F32),
        compiler_params=_params("parallel"),
        name="log_sigmoid_bias",
    )(z, bias.reshape(1, n).astype(F32))


_NTOP = PEER_TOPK + 1
_STAIR = [(a, _NTOP // (a + 1)) for a in range(_NTOP)]
_STAIR_ROWS = ((sum(nb for _, nb in _STAIR) + 7) // 8) * 8


def _peer_score_kernel(q_ref, keys_ref, s1_ref, e1_ref, thr_ref, e2_ref, top_ref, cand_ref, cv_ref):
    q = q_ref[...]
    top_ref[...] = jnp.full(top_ref.shape, -jnp.inf, F32)
    cv_ref[...] = jnp.full(cv_ref.shape, -jnp.inf, F32)
    sc = []
    for c in range(2):
        qc = q[:, c * PEER_NKEYS:(c + 1) * PEER_NKEYS].astype(BF16)
        kc = keys_ref[c].astype(BF16)
        s = lax.dot_general(kc, qc, (((1,), (1,)), ((), ())), preferred_element_type=F32)
        sc.append(s)
        for k in range(_NTOP):
            mx = jnp.max(s, axis=0, keepdims=True)
            top_ref[c, k:k + 1, :] = mx
            s = jnp.where(s == mx, -jnp.inf, s)
    v1 = top_ref[0]
    v2 = top_ref[1]
    cand_ref[...] = jnp.full(cand_ref.shape, -jnp.inf, F32)
    off = 0
    for a, nb in _STAIR:
        cand_ref[off:off + nb, :] = v1[a:a + 1, :] + v2[0:nb, :]
        off += nb
    cand = cand_ref[...]
    for k in range(_NTOP):
        mx = jnp.max(cand, axis=0, keepdims=True)
        cv_ref[k:k + 1, :] = mx
        cand = jnp.where(cand == mx, -jnp.inf, cand)
    cv = cv_ref[...]
    tau = 0.5 * (cv[PEER_TOPK - 1:PEER_TOPK, :] + cv[PEER_TOPK:PEER_TOPK + 1, :])
    z = jnp.sum(jnp.exp(cv[0:PEER_TOPK, :] - cv[0:1, :]), axis=0, keepdims=True)
    s1, s2 = sc
    s1_ref[0] = s1
    e1_ref[0] = jnp.exp(s1 - v1[0:1, :])
    thr_ref[0] = tau - s2
    e2_ref[0] = jnp.exp(s2 - v2[0:1, :]) / z


def peer_scores(q, keys, tm=256):
    t = q.shape[0]
    tm = min(tm, t)
    assert t % tm == 0 and tm % LANES == 0
    shp = jax.ShapeDtypeStruct((PEER_HEADS, PEER_NKEYS, t), F32)
    ospec = pl.BlockSpec((1, PEER_NKEYS, tm), lambda i, h: (h, 0, i))
    return pl.pallas_call(
        _peer_score_kernel,
        grid=(t // tm, PEER_HEADS),
        in_specs=[pl.BlockSpec((tm, 2 * PEER_NKEYS), lambda i, h: (i, h)),
                  pl.BlockSpec(keys.shape, lambda i, h: (0, 0, 0))],
        out_specs=[ospec, ospec, ospec, ospec],
        out_shape=[shp, shp, shp, shp],
        scratch_shapes=[pltpu.VMEM((2, _NTOP + 7, tm), F32), pltpu.VMEM((_STAIR_ROWS, tm), F32),
                        pltpu.VMEM((_NTOP + 7, tm), F32)],
        compiler_params=_params("parallel", "parallel"),
        name="peer_scores",
    )(q, keys)


GATE_ROWS = 16


def _peer_dense_kernel(x_ref, g_ref, u_ref, vt_ref, s1_ref, e1_ref, thr_ref, e2_ref, o_ref,
                       xn_ref, ht_ref, at_ref, acc_ref, srow_ref, *, ci, nc):
    s = pl.program_id(1)
    tm = x_ref.shape[0]

    @pl.when(s == 0)
    def _():
        x = x_ref[...]
        xn_ref[...] = (x * lax.rsqrt(jnp.mean(x * x, axis=-1, keepdims=True) + EPS) * g_ref[...]).astype(BF16)
        acc_ref[...] = jnp.zeros(acc_ref.shape, F32)
        ht_ref[...] = jnp.zeros(ht_ref.shape, F32)
        at_ref[...] = jnp.zeros(at_ref.shape, BF16)

    gate_live = jnp.logical_and(s >= 1, s <= nc)
    chunk = jnp.clip(s - 1, 0, nc - 1)
    slab = pl.ds(pl.multiple_of(chunk * ci, ci), ci)
    for h in range(PEER_HEADS):
        srow_ref[0, h] = s1_ref[h, slab, :]
        srow_ref[1, h] = e1_ref[h, slab, :]

    def stages(p):
        q = 1 - p
        ht_ref[p] = lax.dot_general(u_ref[...], xn_ref[...], (((1,), (1,)), ((), ())),
                                    preferred_element_type=F32)
        for t0 in range(0, tm, LANES):
            for j0 in range(0, PEER_NKEYS, GATE_ROWS):
                gates = [jnp.zeros((GATE_ROWS, LANES), F32) for _ in range(ci)]
                for h in range(PEER_HEADS):
                    thr = thr_ref[h, j0:j0 + GATE_ROWS, t0:t0 + LANES]
                    e2 = e2_ref[h, j0:j0 + GATE_ROWS, t0:t0 + LANES]
                    for i in range(ci):
                        s1 = srow_ref[0, h, i:i + 1, t0:t0 + LANES]
                        e1 = srow_ref[1, h, i:i + 1, t0:t0 + LANES]
                        gates[i] = gates[i] + jnp.where(s1 >= thr, e2, 0.0) * e1
                for i in range(ci):
                    r0 = i * PEER_NKEYS + j0
                    a = gates[i] * jax.nn.gelu(ht_ref[q, r0:r0 + GATE_ROWS, t0:t0 + LANES])
                    at_ref[q, r0:r0 + GATE_ROWS, t0:t0 + LANES] = jnp.where(gate_live, a, 0.0).astype(BF16)
        acc_ref[...] += jnp.dot(vt_ref[...], at_ref[p], preferred_element_type=F32)

    pl.when(s % 2 == 0)(functools.partial(stages, 0))
    pl.when(s % 2 == 1)(functools.partial(stages, 1))

    @pl.when(s == nc + 1)
    def _():
        o_ref[...] = x_ref[...] + acc_ref[...].T


def peer_dense(x, gain, u_all, vt_all, layer, s1, e1, thr, e2, tm=512, ci=8):
    t, d = x.shape
    ne = u_all.shape[1]
    tm = min(tm, t)
    ec = ci * PEER_NKEYS
    assert t % tm == 0 and ne % ec == 0 and ci == 8
    nc = ne // ec
    once = pl.Buffered(1)
    sspec = pl.BlockSpec((PEER_HEADS, PEER_NKEYS, tm), lambda i, s: (0, 0, i), pipeline_mode=once)
    return pl.pallas_call(
        functools.partial(_peer_dense_kernel, ci=ci, nc=nc),
        grid=(t // tm, nc + 2),
        in_specs=[pl.BlockSpec((tm, d), lambda i, s: (i, 0), pipeline_mode=once),
                  pl.BlockSpec((1, d), lambda i, s: (0, 0)),
                  pl.BlockSpec((None, ec, d), lambda i, s: (layer, jnp.minimum(s, nc - 1), 0)),
                  pl.BlockSpec((None, d, ec), lambda i, s: (layer, 0, jnp.maximum(s - 2, 0))),
                  sspec, sspec, sspec, sspec],
        out_specs=pl.BlockSpec((tm, d), lambda i, s: (i, 0)),
        out_shape=jax.ShapeDtypeStruct((t, d), F32),
        scratch_shapes=[pltpu.VMEM((tm, d), BF16), pltpu.VMEM((2, ec, tm), F32),
                        pltpu.VMEM((2, ec, tm), BF16), pltpu.VMEM((d, tm), F32),
                        pltpu.VMEM((2, PEER_HEADS, ci, tm), F32)],
        compiler_params=_params("parallel", "arbitrary"),
        name="peer_dense",
    )(x, gain.reshape(1, d).astype(F32), u_all, vt_all, s1, e1, thr, e2)


def peer_layer(x, gain, wq_bf, keys, u_all, vt_all, layer):
    t = x.shape[0]
    tp = -(-t // LANES) * LANES
    xp = jnp.pad(x, ((0, tp - t), (0, 0))) if tp != t else x
    q = rms_matmul(xp, gain, wq_bf)
    s1, e1, thr, e2 = peer_scores(q, keys)
    out = peer_dense(xp, gain, u_all, vt_all, layer, s1, e1, thr, e2)
    return out[:t]


def kernel(x_prompt, x_sample, mem_prompt, cache_diff_k, cache_diff_v, cache_fox_k, cache_fox_v, cache_fox_logf, state_conv, cache_mem_k, cache_mem_v, page_table, norm_mix, norm_xattn, norm_ffn, even_w_in, even_w_out, diff_q_gain, diff_k_gain, diff_lambda, diff_out_gain, conv_w, conv_b, conv_ln_g, conv_ln_b, odd_w_in, fox_forget_bias, fox_q_gain, fox_k_gain, odd_w_out, mem_norm, xattn_wq, xattn_wk, xattn_wv, xattn_wo, xattn_q_gain, xattn_k_gain, peer_wq, peer_keys, peer_u, peer_v):
    bp, seq, d = x_prompt.shape
    bs = x_sample.shape[0]
    depth = norm_mix.shape[0]
    half = d // 2
    n_pages = page_table.shape[1]
    past = n_pages * PAGE
    n_phys = cache_diff_k.shape[1]
    mem_len = mem_prompt.shape[1]
    xw = xattn_wq.shape[2]
    tp = bp * seq
    srow = 8

    xp = x_prompt.reshape(tp, d)
    xs = x_sample.reshape(bs, d)
    memf = mem_prompt.reshape(bp * mem_len, d)

    n_even, n_odd = cache_diff_k.shape[0], cache_fox_k.shape[0]
    dk_t = jnp.transpose(cache_diff_k, (0, 1, 3, 4, 5, 2)).reshape(n_even * n_phys, DEC_ROWS, DIFF_DK, PAGE)
    dv_c = cache_diff_v.reshape(n_even * n_phys, PAGE * DIFF_HEADS, LANES)
    fk_c = cache_fox_k.reshape(n_odd * n_phys, PAGE * FOX_HEADS, LANES)
    fv_c = cache_fox_v.reshape(n_odd * n_phys, PAGE * FOX_HEADS, LANES)
    lf_t = jnp.swapaxes(cache_fox_logf, 2, 3).reshape(n_odd * n_phys, FOX_HEADS, PAGE)

    u_all = peer_u.astype(BF16)
    vt_all = jnp.swapaxes(peer_v.astype(BF16), 1, 2)

    rope_p = rope_tables(jnp.arange(seq, dtype=I32), seq)
    rope_s = rope_tables(jnp.full((1,), past, I32), bs)

    outs = {k: [] for k in ("dk_p", "dv_p", "dk_s", "dv_s", "fk_p", "fv_p", "fl_p", "fk_s", "fv_s", "fl_s",
                            "cv_p", "cv_s", "mk_p", "mv_p")}

    for l in range(depth):
        i = l // 2
        if l % 2 == 0:
            lam_init = 0.8 - 0.6 * math.exp(-0.3 * l)
            w_in = even_w_in[i].astype(BF16)
            w_out = even_w_out[i].astype(BF16)
            nq = half // LANES
            res = []
            for x, rope in ((xp, rope_p), (xs, rope_s)):
                z = rms_matmul(x, norm_mix[l], w_in)
                qn = head_norm(z, 0, nq, diff_q_gain[i], DIFF_DK, rope)
                kn = head_norm(z, nq, nq, diff_k_gain[i], DIFF_DK, rope)
                v = z[:, 2 * half:3 * half]
                u = glu(z, 3 * half, 4 * half, half)
                res.append((z, qn, kn, v, u))
            z, qn, kn, v, u = res[0]
            attn = flash_attention(qn, 0, kn, 0, z, 2 * nq, bsz=bp, heads=DIFF_HEADS, sq=seq, sk=seq,
                                   mode="diff", causal=True, scale=DIFF_DK ** -0.5, tq=512, tk=512,
                                   lam_params=diff_lambda[i], lam_init=lam_init)
            attn = head_norm(attn, 0, nq, diff_out_gain[i], LANES, out_scale=1.0 - lam_init)
            u3 = u.reshape(bp, seq, half)
            conv = conv_branch(u3, jnp.zeros((bp, CONV_HALO, half), F32), conv_w[i], conv_b[i],
                               conv_ln_g[i], conv_ln_b[i]).reshape(tp, half)
            xp = rms_matmul(jnp.concatenate([attn, conv], axis=1), None, w_out, res=xp)
            outs["dk_p"].append(kn.reshape(bp, seq, DIFF_HEADS, 2, DIFF_DK))
            outs["dv_p"].append(v.reshape(bp, seq, DIFF_HEADS, 2 * DIFF_DK))
            outs["cv_p"].append(u3[:, seq - (CONV_WIDTH - 1):])
            z, qn, kn, v, u = res[1]
            attn = decode_attention(qn.reshape(bs, DEC_ROWS, DIFF_DK), kn.reshape(bs, DEC_ROWS, DIFF_DK),
                                    v.reshape(bs, DIFF_HEADS, LANES), dk_t, dv_c, page_table, i * n_phys,
                                    mode="diff", scale=DIFF_DK ** -0.5,
                                    lam_params=diff_lambda[i], lam_init=lam_init).reshape(bs, half)
            attn = head_norm(attn, 0, nq, diff_out_gain[i], LANES, out_scale=1.0 - lam_init)
            st = state_conv[i]
            init = jnp.concatenate([jnp.zeros((bs, CONV_HALO - (CONV_WIDTH - 1), half), F32), st], axis=1)
            u_pad = jnp.concatenate([u[:, None, :], jnp.zeros((bs, srow - 1, half), F32)], axis=1)
            conv = conv_branch(u_pad, init, conv_w[i], conv_b[i], conv_ln_g[i], conv_ln_b[i])[:, 0]
            xs = rms_matmul(jnp.concatenate([attn, conv], axis=1), None, w_out, res=xs)
            outs["dk_s"].append(kn.reshape(bs, 1, DIFF_HEADS, 2, DIFF_DK))
            outs["dv_s"].append(v.reshape(bs, 1, DIFF_HEADS, 2 * DIFF_DK))
            outs["cv_s"].append(jnp.concatenate([st[:, 1:], u[:, None, :]], axis=1))
        else:
            w_in = odd_w_in[i]
            w_main = w_in[:, :3 * d].astype(BF16)
            w_gate = jnp.pad(w_in[:, 3 * d:], ((0, 0), (0, LANES - FOX_HEADS))).astype(BF16)
            w_out = odd_w_out[i].astype(BF16)
            nh = d // LANES
            fbias = jnp.pad(fox_forget_bias[i], (0, LANES - FOX_HEADS))
            res = []
            for x in (xp, xs):
                z = rms_matmul(x, norm_mix[l], w_main)
                zg = rms_matmul(x, norm_mix[l], w_gate)
                qn = head_norm(z, 0, nh, fox_q_gain[i], LANES)
                kn = head_norm(z, nh, nh, fox_k_gain[i], LANES)
                v = z[:, 2 * d:3 * d]
                logf = log_sigmoid_bias(zg, fbias)[:, :FOX_HEADS]
                res.append((z, qn, kn, v, logf))
            z, qn, kn, v, logf = res[0]
            cum = cumsum_time(logf.reshape(bp, seq, FOX_HEADS))
            cum_t = jnp.swapaxes(cum, 1, 2).reshape(bp * FOX_HEADS, 1, seq)
            attn = flash_attention(qn, 0, kn, 0, z, 2 * nh, bsz=bp, heads=FOX_HEADS, sq=seq, sk=seq,
                                   mode="fox", causal=True, scale=LANES ** -0.5, tq=1024, tk=512, ck=cum_t)
            xp = rms_matmul(attn, None, w_out, res=xp)
            outs["fk_p"].append(kn.reshape(bp, seq, FOX_HEADS, LANES))
            outs["fv_p"].append(v.reshape(bp, seq, FOX_HEADS, LANES))
            outs["fl_p"].append(logf.reshape(bp, seq, FOX_HEADS))
            z, qn, kn, v, logf = res[1]
            h3 = (bs, FOX_HEADS, LANES)
            attn = decode_attention(qn.reshape(h3), kn.reshape(h3), v.reshape(h3), fk_c, fv_c, page_table,
                                    i * n_phys, mode="fox", scale=LANES ** -0.5,
                                    lf_cache_t=lf_t, lf_new=logf).reshape(bs, d)
            xs = rms_matmul(attn, None, w_out, res=xs)
            outs["fk_s"].append(kn.reshape(bs, 1, FOX_HEADS, LANES))
            outs["fv_s"].append(v.reshape(bs, 1, FOX_HEADS, LANES))
            outs["fl_s"].append(logf.reshape(bs, 1, FOX_HEADS))

        wq = xattn_wq[l].astype(BF16)
        wo = xattn_wo[l].astype(BF16)
        wkv = jnp.concatenate([xattn_wk[l], xattn_wv[l]], axis=1).astype(BF16)
        nxh = xw // LANES
        kv = rms_matmul(memf, mem_norm[l], wkv)
        mk = head_norm(kv, 0, nxh, xattn_k_gain[l], LANES)
        mv = kv[:, xw:]
        outs["mk_p"].append(mk.reshape(bp, mem_len, X_HEADS, LANES))
        outs["mv_p"].append(mv.reshape(bp, mem_len, X_HEADS, LANES))
        xq = head_norm(rms_matmul(xp, norm_xattn[l], wq), 0, nxh, xattn_q_gain[l], LANES)
        o = flash_attention(xq, 0, mk, 0, kv, nxh, bsz=bp, heads=X_HEADS, sq=seq, sk=mem_len,
                            mode="plain", causal=False, scale=LANES ** -0.5, tq=1024, tk=mem_len)
        xp = rms_matmul(o, None, wo, res=xp)
        xq = head_norm(rms_matmul(xs, norm_xattn[l], wq), 0, nxh, xattn_q_gain[l], LANES)
        xq = jnp.pad(xq[:, None, :], ((0, 0), (0, srow - 1), (0, 0))).reshape(bs * srow, xw)
        o = flash_attention(xq, 0, cache_mem_k[l].reshape(bs * mem_len, xw), 0,
                            cache_mem_v[l].reshape(bs * mem_len, xw), 0, bsz=bs, heads=X_HEADS,
                            sq=srow, sk=mem_len, mode="plain", causal=False, scale=LANES ** -0.5,
                            tq=srow, tk=mem_len)
        o = o.reshape(bs, srow, xw)[:, 0]
        xs = rms_matmul(o, None, wo, res=xs)

        pwq = peer_wq[l].astype(BF16)
        xp = peer_layer(xp, norm_ffn[l], pwq, peer_keys[l], u_all, vt_all, l)
        xs = peer_layer(xs, norm_ffn[l], pwq, peer_keys[l], u_all, vt_all, l)

    st = lambda k: jnp.stack(outs[k])
    return (xp.reshape(bp, seq, d), xs.reshape(bs, 1, d),
            st("dk_p"), st("dv_p"), st("fk_p"), st("fv_p"), st("fl_p"),
            st("cv_p"), st("mk_p"), st("mv_p"),
            st("dk_s"), st("dv_s"), st("fk_s"), st("fv_s"), st("fl_s"), st("cv_s"))
```

```python
import functools
import math

import jax
import jax.numpy as jnp
from jax import lax
from jax.experimental import pallas as pl
from jax.experimental.pallas import tpu as pltpu

F32 = jnp.float32
BF16 = jnp.bfloat16
I32 = jnp.int32

EPS = 1e-6
NEG = -1e30
ROPE_THETA = 500000.0
LANES = 128
MXU_WIDTH = 256
V7X_VMEM_BYTES = 64 * 1024 * 1024
VMEM_LIMIT = V7X_VMEM_BYTES - 8 * 1024 * 1024

DIFF_HEADS = 8
DIFF_DK = 64
DIFF_ROT = 16
FOX_HEADS = 16
X_HEADS = 4
PEER_HEADS = 8
PEER_NKEYS = 128
PEER_TOPK = 16
CONV_WIDTH = 31
CONV_HALO = 32
PAGE = 128


def _params(*sem):
    return pltpu.CompilerParams(dimension_semantics=sem, vmem_limit_bytes=VMEM_LIMIT)


def _rms_matmul_kernel(*refs, use_norm, has_res):
    if has_res:
        x_ref, g_ref, w_ref, r_ref, o_ref, xn_ref = refs
    else:
        x_ref, g_ref, w_ref, o_ref, xn_ref = refs

    @pl.when(pl.program_id(1) == 0)
    def _():
        x = x_ref[...]
        if use_norm:
            x = x * lax.rsqrt(jnp.mean(x * x, axis=-1, keepdims=True) + EPS) * g_ref[...]
        xn_ref[...] = x.astype(BF16)

    acc = jnp.dot(xn_ref[...], w_ref[...], preferred_element_type=F32)
    if has_res:
        acc = r_ref[...] + acc
    o_ref[...] = acc


def rms_matmul(x, gain, w, res=None, tm=1024, tn=512):
    m, k = x.shape
    n = w.shape[1]
    tm = min(tm, m)
    tn = min(tn, n)
    assert m % tm == 0 and n % tn == 0, (m, n, tm, tn)
    use_norm = gain is not None
    g = (gain if use_norm else jnp.ones((k,), F32)).reshape(1, k).astype(F32)
    in_specs = [pl.BlockSpec((tm, k), lambda i, j: (i, 0)),
                pl.BlockSpec((1, k), lambda i, j: (0, 0)),
                pl.BlockSpec((k, tn), lambda i, j: (0, j))]
    args = [x, g, w]
    if res is not None:
        in_specs.append(pl.BlockSpec((tm, tn), lambda i, j: (i, j)))
        args.append(res)
    return pl.pallas_call(
        functools.partial(_rms_matmul_kernel, use_norm=use_norm, has_res=res is not None),
        grid=(m // tm, n // tn),
        in_specs=in_specs,
        out_specs=pl.BlockSpec((tm, tn), lambda i, j: (i, j)),
        out_shape=jax.ShapeDtypeStruct((m, n), F32),
        scratch_shapes=[pltpu.VMEM((tm, k), BF16)],
        compiler_params=_params("parallel", "arbitrary"),
        name="rms_matmul",
    )(*args)


def _head_norm_kernel(*refs, gw, rope, out_scale):
    if rope:
        a_ref, g_ref, cos_ref, sin_ref, o_ref = refs
    else:
        a_ref, g_ref, o_ref = refs
    x = a_ref[...]
    xx = x * x
    lane = lax.broadcasted_iota(I32, x.shape, 1)
    shift = int(math.log2(gw))
    gr = lax.broadcasted_iota(I32, (LANES, LANES), 0) >> shift
    gc = lax.broadcasted_iota(I32, (LANES, LANES), 1) >> shift
    gm = jnp.where(gr == gc, 1.0 / gw, 0.0).astype(BF16)
    hi = xx.astype(BF16)
    lo = (xx - hi.astype(F32)).astype(BF16)
    ms = jnp.dot(hi, gm, preferred_element_type=F32) + jnp.dot(lo, gm, preferred_element_type=F32)
    y = x * lax.rsqrt(ms + EPS) * g_ref[...]
    if rope:
        half = DIFF_ROT // 2
        first = (lane & (gw - 1)) < half
        partner = jnp.where(first, pltpu.roll(y, LANES - half, 1), pltpu.roll(y, half, 1))
        y = y * cos_ref[...] + partner * sin_ref[...]
    if out_scale != 1.0:
        y = y * out_scale
    o_ref[...] = y


def head_norm(a, col0, ncols, gain, gw, rope_tabs=None, out_scale=1.0, tm=1024):
    m = a.shape[0]
    tm = min(tm, m)
    if rope_tabs is not None:
        tm = min(tm, rope_tabs[0].shape[0])
    assert m % tm == 0
    g = jnp.tile(gain.astype(F32), LANES // gw).reshape(1, LANES)
    in_specs = [pl.BlockSpec((tm, LANES), lambda i, j: (i, col0 + j)),
                pl.BlockSpec((1, LANES), lambda i, j: (0, 0))]
    args = [a, g]
    if rope_tabs is not None:
        cos_t, sin_t = rope_tabs
        nrep = cos_t.shape[0] // tm
        in_specs += [pl.BlockSpec((tm, LANES), lambda i, j: (i % nrep, 0)),
                     pl.BlockSpec((tm, LANES), lambda i, j: (i % nrep, 0))]
        args += [cos_t, sin_t]
    return pl.pallas_call(
        functools.partial(_head_norm_kernel, gw=gw, rope=rope_tabs is not None, out_scale=out_scale),
        grid=(m // tm, ncols),
        in_specs=in_specs,
        out_specs=pl.BlockSpec((tm, LANES), lambda i, j: (i, j)),
        out_shape=jax.ShapeDtypeStruct((m, ncols * LANES), F32),
        compiler_params=_params("parallel", "parallel"),
        name="head_norm",
    )(*args)


def rope_tables(pos, rows):
    half = DIFF_ROT // 2
    inv = jnp.power(ROPE_THETA, -jnp.arange(half, dtype=F32) * 2.0 / DIFF_ROT)
    ang = pos.astype(F32)[:, None] * inv[None, :]
    cos, sin = jnp.cos(ang), jnp.sin(ang)
    ones = jnp.ones((pos.shape[0], DIFF_DK - DIFF_ROT), F32)
    cos64 = jnp.concatenate([cos, cos, ones], axis=1)
    sin64 = jnp.concatenate([-sin, sin, 0.0 * ones], axis=1)
    cos_t = jnp.tile(cos64, (1, LANES // DIFF_DK))
    sin_t = jnp.tile(sin64, (1, LANES // DIFF_DK))
    if cos_t.shape[0] != rows:
        cos_t = jnp.broadcast_to(cos_t[:1], (rows, LANES))
        sin_t = jnp.broadcast_to(sin_t[:1], (rows, LANES))
    return cos_t, sin_t


def _glu_kernel(a_ref, b_ref, o_ref):
    o_ref[...] = a_ref[...] * jax.nn.sigmoid(b_ref[...])


def glu(z, cola, colb, width, tm=1024, tn=512):
    m = z.shape[0]
    tm = min(tm, m)
    nb = width // tn
    return pl.pallas_call(
        _glu_kernel,
        grid=(m // tm, nb),
        in_specs=[pl.BlockSpec((tm, tn), lambda i, j: (i, cola // tn + j)),
                  pl.BlockSpec((tm, tn), lambda i, j: (i, colb // tn + j))],
        out_specs=pl.BlockSpec((tm, tn), lambda i, j: (i, j)),
        out_shape=jax.ShapeDtypeStruct((m, width), F32),
        compiler_params=_params("parallel", "parallel"),
        name="glu",
    )(z, z)


def _conv_kernel(cur_ref, prev_ref, init_ref, w_ref, b_ref, g_ref, beta_ref, o_ref, buf_ref, *, tt, rc):
    i = pl.program_id(1)

    @pl.when(i == 0)
    def _():
        buf_ref[0:CONV_HALO, :] = init_ref[0]

    if tt >= CONV_HALO:
        @pl.when(i > 0)
        def _():
            buf_ref[0:CONV_HALO, :] = prev_ref[0, tt - CONV_HALO:tt, :]

    buf_ref[CONV_HALO:CONV_HALO + tt, :] = cur_ref[0]
    off = CONV_HALO - (CONV_WIDTH - 1)
    for r0 in range(0, tt, rc):
        acc = jnp.zeros((rc, cur_ref.shape[2]), F32)
        for j in range(CONV_WIDTH):
            acc = acc + w_ref[j:j + 1, :] * buf_ref[r0 + off + j:r0 + off + j + rc, :]
        y = acc + b_ref[...]
        mu = jnp.mean(y, axis=-1, keepdims=True)
        yc = y - mu
        var = jnp.mean(yc * yc, axis=-1, keepdims=True)
        y = yc * lax.rsqrt(var + EPS) * g_ref[...] + beta_ref[...]
        o_ref[0, r0:r0 + rc, :] = y * jax.nn.sigmoid(y)


def conv_branch(u, init, w, b, ln_g, ln_b, tt=128):
    bsz, length, ch = u.shape
    tt = min(tt, length)
    assert length % tt == 0 and (tt >= CONV_HALO or length == tt)
    rc = min(16, tt)
    wpad = jnp.concatenate([w, jnp.zeros((CONV_HALO - CONV_WIDTH, ch), F32)], axis=0)
    row = lambda v: v.reshape(1, ch).astype(F32)
    return pl.pallas_call(
        functools.partial(_conv_kernel, tt=tt, rc=rc),
        grid=(bsz, length // tt),
        in_specs=[pl.BlockSpec((1, tt, ch), lambda bi, i: (bi, i, 0)),
                  pl.BlockSpec((1, tt, ch), lambda bi, i: (bi, jnp.maximum(i - 1, 0), 0)),
                  pl.BlockSpec((1, CONV_HALO, ch), lambda bi, i: (bi, 0, 0)),
                  pl.BlockSpec((CONV_HALO, ch), lambda bi, i: (0, 0)),
                  pl.BlockSpec((1, ch), lambda bi, i: (0, 0)),
                  pl.BlockSpec((1, ch), lambda bi, i: (0, 0)),
                  pl.BlockSpec((1, ch), lambda bi, i: (0, 0))],
        out_specs=pl.BlockSpec((1, tt, ch), lambda bi, i: (bi, i, 0)),
        out_shape=jax.ShapeDtypeStruct((bsz, length, ch), F32),
        scratch_shapes=[pltpu.VMEM((CONV_HALO + tt, ch), F32)],
        compiler_params=_params("parallel", "arbitrary"),
        name="conv_branch",
    )(u, u, init, wpad, row(b), row(ln_g), row(ln_b))


LOG2E = 1.4426950408889634


def _flash_kernel(*refs, mode, causal, qscale, tq, tk, lam_init):
    qi_ref, ki_ref, last_ref = refs[:3]
    if mode == "fox":
        q_ref, k_ref, v_ref, ck_ref, o_ref, qs_ref, m_ref, l_ref, acc_ref = refs[3:]
    elif mode == "diff":
        q_ref, k_ref, v_ref, lam_ref, o_ref, qs_ref, m_ref, l_ref, acc_ref = refs[3:]
    else:
        q_ref, k_ref, v_ref, o_ref, qs_ref, m_ref, l_ref, acc_ref = refs[3:]
    step = pl.program_id(2)
    qi = qi_ref[step]
    ki = ki_ref[step]

    @pl.when(ki == 0)
    def _():
        q = q_ref[...] * qscale
        if mode == "diff":
            lane = lax.broadcasted_iota(I32, q.shape, 1)
            lo = lane < DIFF_DK
            qs_ref[0:tq, :] = jnp.where(lo, q, 0.0).astype(BF16)
            qs_ref[tq:2 * tq, :] = jnp.where(lo, 0.0, q).astype(BF16)
        else:
            qs_ref[...] = q.astype(BF16)
        m_ref[...] = jnp.full(m_ref.shape, NEG, F32)
        l_ref[...] = jnp.zeros(l_ref.shape, F32)
        acc_ref[...] = jnp.zeros(acc_ref.shape, F32)

    def compute(masked):
        s = lax.dot_general(qs_ref[...], k_ref[...].astype(BF16), (((1,), (1,)), ((), ())),
                            preferred_element_type=F32)
        if mode == "fox":
            s = s - ck_ref[0] * LOG2E
        if masked:
            r = lax.broadcasted_iota(I32, s.shape, 0)
            if mode == "diff":
                r = jnp.where(r >= tq, r - tq, r)
            c = lax.broadcasted_iota(I32, s.shape, 1)
            s = jnp.where(ki * tk + c <= qi * tq + r, s, NEG)
        m_prev = m_ref[...]
        m_next = jnp.maximum(m_prev, jnp.max(s, axis=1, keepdims=True))
        alpha = jnp.exp2(m_prev - m_next)
        p = jnp.exp2(s - jnp.concatenate([m_next] * (tk // LANES), axis=1))
        l_ref[...] = alpha * l_ref[...] + jnp.sum(p, axis=1, keepdims=True)
        acc_ref[...] = alpha * acc_ref[...] + jnp.dot(p.astype(BF16), v_ref[...].astype(BF16),
                                                      preferred_element_type=F32)
        m_ref[...] = m_next

    if causal:
        crosses = ki * tk + tk - 1 > qi * tq
        pl.when(crosses)(functools.partial(compute, True))
        pl.when(jnp.logical_not(crosses))(functools.partial(compute, False))
    else:
        compute(False)

    @pl.when(last_ref[step] == 1)
    def _():
        o = acc_ref[...] / l_ref[...]
        if mode == "diff":
            lp = lam_ref[...]
            lam = (jnp.exp(jnp.sum(lp[0:1] * lp[1:2], axis=-1, keepdims=True))
                   - jnp.exp(jnp.sum(lp[2:3] * lp[3:4], axis=-1, keepdims=True)) + lam_init)
            o = o[0:tq] - lam * o[tq:2 * tq]
        o_ref[...] = o


def flash_attention(qa, qc0, ka, kc0, va, vc0, *, bsz, heads, sq, sk, mode, causal, scale, tq, tk,
                    ck=None, lam_params=None, lam_init=0.0):
    tq = min(tq, sq)
    tk = min(tk, sk)
    nq, nk = sq // tq, sk // tk
    assert sq % tq == 0 and sk % tk == 0 and tk % LANES == 0
    rows = 2 * tq if mode == "diff" else tq
    pairs = []
    for qi in range(nq):
        kmax = (qi * tq + tq - 1) // tk if causal else nk - 1
        pairs += [(qi, ki, int(ki == kmax)) for ki in range(kmax + 1)]
    qi_l, ki_l, last_l = (jnp.asarray(v, I32) for v in zip(*pairs))

    in_specs = [pl.BlockSpec((tq, LANES), lambda b, h, p, qr, kr, lr: (b * nq + qr[p], qc0 + h)),
                pl.BlockSpec((tk, LANES), lambda b, h, p, qr, kr, lr: (b * nk + kr[p], kc0 + h)),
                pl.BlockSpec((tk, LANES), lambda b, h, p, qr, kr, lr: (b * nk + kr[p], vc0 + h))]
    args = [qa, ka, va]
    scratch = [pltpu.VMEM((rows, LANES), BF16), pltpu.VMEM((rows, LANES), F32),
               pltpu.VMEM((rows, LANES), F32), pltpu.VMEM((rows, LANES), F32)]
    if mode == "fox":
        in_specs.append(pl.BlockSpec((1, 1, tk), lambda b, h, p, qr, kr, lr: (b * heads + h, 0, kr[p])))
        args.append(ck)
    elif mode == "diff":
        in_specs.append(pl.BlockSpec(lam_params.shape, lambda b, h, p, qr, kr, lr: (0, 0)))
        args.append(lam_params)
    return pl.pallas_call(
        functools.partial(_flash_kernel, mode=mode, causal=causal, qscale=scale * LOG2E, tq=tq, tk=tk,
                          lam_init=lam_init),
        grid_spec=pltpu.PrefetchScalarGridSpec(
            num_scalar_prefetch=3,
            grid=(bsz, heads, len(pairs)),
            in_specs=in_specs,
            out_specs=pl.BlockSpec((tq, LANES), lambda b, h, p, qr, kr, lr: (b * nq + qr[p], h)),
            scratch_shapes=scratch),
        out_shape=jax.ShapeDtypeStruct((bsz * sq, heads * LANES), F32),
        compiler_params=_params("parallel", "parallel", "arbitrary"),
        name="flash_" + mode,
    )(qi_l, ki_l, last_l, *args)


DEC_ROWS = 16


def _split3(x):
    hi = x.astype(BF16)
    r1 = x - hi.astype(F32)
    mid = r1.astype(BF16)
    lo = (r1 - mid.astype(F32)).astype(BF16)
    return hi, mid, lo


def _decode_kernel(*refs, mode, scale, vheads, pps, lam_init):
    refs = refs[1:]
    if mode == "fox":
        q_ref, kn_ref, vn_ref, te_ref = refs[:4]
        kc_refs, vc_refs, lf_refs = refs[4:4 + pps], refs[4 + pps:4 + 2 * pps], refs[4 + 2 * pps:4 + 3 * pps]
        lfn_ref, o_ref, qb_ref, m_ref, l_ref, acc_ref, carry_ref = refs[4 + 3 * pps:]
    else:
        q_ref, kn_ref, qrep_ref, vn_ref = refs[:4]
        kc_refs, vc_refs = refs[4:4 + pps], refs[4 + pps:4 + 2 * pps]
        lam_ref, o_ref, m_ref, l_ref, acc_ref = refs[4 + 2 * pps:]
    j = pl.program_id(1)
    nsteps = pl.num_programs(1)
    rid = lax.broadcasted_iota(I32, (DEC_ROWS, LANES), 0)

    def lanes_x(x, k):
        return jnp.concatenate([x] * k, axis=1) if k > 1 else x

    @pl.when(j == 0)
    def _():
        q = q_ref[0]
        s_new = jnp.sum(q * kn_ref[0], axis=-1, keepdims=True) * scale
        m_ref[...] = jnp.broadcast_to(s_new, m_ref.shape)
        l_ref[...] = jnp.ones(l_ref.shape, F32)
        acc_ref[...] = vn_ref[0]
        if mode == "fox":
            qb_ref[...] = q.astype(BF16)
            carry_ref[...] = jnp.broadcast_to(lfn_ref[0], carry_ref.shape)

    for g in range(pps):
        if mode == "fox":
            wide = PAGE * DEC_ROWS // LANES
            s = lax.dot_general(qb_ref[...], kc_refs[g][0].astype(BF16), (((1,), (1,)), ((), ())),
                                preferred_element_type=F32) * scale
            lf = lf_refs[g][0]
            hi, mid, lo = _split3(lf)
            b3 = jnp.dot(jnp.concatenate([hi, mid, lo], axis=0), te_ref[...], preferred_element_type=F32)
            suffix = b3[0:DEC_ROWS] + b3[DEC_ROWS:2 * DEC_ROWS] + b3[2 * DEC_ROWS:3 * DEC_ROWS]
            col = lax.broadcasted_iota(I32, s.shape, 1)
            row = lax.broadcasted_iota(I32, s.shape, 0)
            s = jnp.where((col & (DEC_ROWS - 1)) == row, s + (suffix + lanes_x(carry_ref[...], wide)), NEG)
            carry_ref[...] = carry_ref[...] + jnp.sum(lf, axis=-1, keepdims=True)
        else:
            wide = 1
            s = jnp.sum(kc_refs[g][0] * qrep_ref[0], axis=1) * scale
        m_prev = m_ref[...]
        m_next = jnp.maximum(m_prev, jnp.max(s, axis=1, keepdims=True))
        alpha = jnp.exp(m_prev - m_next)
        p = jnp.exp(s - lanes_x(m_next, wide))
        l_ref[...] = alpha * l_ref[...] + jnp.sum(p, axis=1, keepdims=True)
        pb = p.astype(BF16)
        if mode == "fox":
            pv = jnp.dot(pb, vc_refs[g][0].astype(BF16), preferred_element_type=F32)
        else:
            pv = jnp.zeros((DEC_ROWS, LANES), F32)
            for h in range(vheads):
                vh = vc_refs[g][0, pl.ds(h, PAGE, stride=vheads), :].astype(BF16)
                pv = jnp.where((rid >> 1) == h, jnp.dot(pb, vh, preferred_element_type=F32), pv)
        acc_ref[...] = alpha * acc_ref[...] + pv
        m_ref[...] = m_next

    @pl.when(j == nsteps - 1)
    def _():
        inv = 1.0 / l_ref[...]
        if mode == "fox":
            o_ref[0] = acc_ref[...] * inv
        else:
            lp = lam_ref[...]
            lam = (jnp.exp(jnp.sum(lp[0:1] * lp[1:2], axis=-1, keepdims=True))
                   - jnp.exp(jnp.sum(lp[2:3] * lp[3:4], axis=-1, keepdims=True)) + lam_init)
            acc_ref[...] = acc_ref[...] * jnp.where((rid & 1) == 0, inv, -lam * inv)
            o_ref[0] = (acc_ref[pl.ds(0, DEC_ROWS // 2, stride=2), :]
                        + acc_ref[pl.ds(1, DEC_ROWS // 2, stride=2), :])


def decode_attention(q, k_new, v_new, k_cache, v_cache, page_table, page_base, *, mode, scale,
                     lf_cache_t=None, lf_new=None, lam_params=None, lam_init=0.0, pps=4):
    bsz = q.shape[0]
    npages = page_table.shape[1]
    assert npages % pps == 0
    pt = page_table.reshape(-1).astype(I32)
    vheads = v_cache.shape[1] // PAGE

    def page(g, ndim):
        return lambda b, j, pt_ref: ((page_base + pt_ref[b * npages + (npages - 1 - (j * pps + g))],)
                                     + (0,) * (ndim - 1))

    per_b = lambda shape: pl.BlockSpec((1,) + shape, lambda b, j, pt_ref: (b,) + (0,) * len(shape))
    kspecs = [pl.BlockSpec((1,) + k_cache.shape[1:], page(g, k_cache.ndim)) for g in range(pps)]
    vspecs = [pl.BlockSpec((1,) + v_cache.shape[1:], page(g, v_cache.ndim)) for g in range(pps)]
    if mode == "fox":
        te = (jnp.arange(PAGE, dtype=I32)[:, None] > (jnp.arange(PAGE * DEC_ROWS, dtype=I32)[None, :] // DEC_ROWS))
        te = te.astype(BF16)
        in_specs = ([per_b(q.shape[1:]), per_b(k_new.shape[1:]), per_b(v_new.shape[1:]),
                     pl.BlockSpec(te.shape, lambda b, j, pt_ref: (0, 0))] + kspecs + vspecs
                    + [pl.BlockSpec((1, DEC_ROWS, PAGE), page(g, 3)) for g in range(pps)]
                    + [per_b((DEC_ROWS, 1))])
        args = ([q, k_new, v_new, te] + [k_cache] * pps + [v_cache] * pps + [lf_cache_t] * pps
                + [lf_new.reshape(bsz, DEC_ROWS, 1)])
        scratch = [pltpu.VMEM((DEC_ROWS, LANES), BF16)]
        out_rows = DEC_ROWS
    else:
        qrep = jnp.broadcast_to(q[..., None], q.shape + (PAGE,))
        v_rows = jnp.repeat(v_new, 2, axis=1)
        in_specs = ([per_b(q.shape[1:]), per_b(k_new.shape[1:]), per_b(qrep.shape[1:]), per_b(v_rows.shape[1:])]
                    + kspecs + vspecs + [pl.BlockSpec(lam_params.shape, lambda b, j, pt_ref: (0, 0))])
        args = [q, k_new, qrep, v_rows] + [k_cache] * pps + [v_cache] * pps + [lam_params]
        scratch = []
        out_rows = DEC_ROWS // 2
    scratch += [pltpu.VMEM((DEC_ROWS, LANES), F32), pltpu.VMEM((DEC_ROWS, LANES), F32),
                pltpu.VMEM((DEC_ROWS, LANES), F32)]
    if mode == "fox":
        scratch.append(pltpu.VMEM((DEC_ROWS, LANES), F32))
    return pl.pallas_call(
        functools.partial(_decode_kernel, mode=mode, scale=scale, vheads=vheads, pps=pps, lam_init=lam_init),
        grid_spec=pltpu.PrefetchScalarGridSpec(
            num_scalar_prefetch=1,
            grid=(bsz, npages // pps),
            in_specs=in_specs,
            out_specs=pl.BlockSpec((1, out_rows, LANES), lambda b, j, pt_ref: (b, 0, 0)),
            scratch_shapes=scratch),
        out_shape=jax.ShapeDtypeStruct((bsz, out_rows, LANES), F32),
        compiler_params=_params("parallel", "arbitrary"),
        name="decode_" + mode,
    )(pt, *args)


def _cumsum_kernel(x_ref, o_ref, carry_ref, *, tt):
    @pl.when(pl.program_id(1) == 0)
    def _():
        carry_ref[...] = jnp.zeros(carry_ref.shape, F32)

    x = x_ref[0]
    r = lax.broadcasted_iota(I32, (tt, tt), 0)
    c = lax.broadcasted_iota(I32, (tt, tt), 1)
    tri = jnp.where(c <= r, 1.0, 0.0).astype(BF16)
    hi, mid, lo = _split3(x)
    cs = (jnp.dot(tri, hi, preferred_element_type=F32) + jnp.dot(tri, mid, preferred_element_type=F32)
          + jnp.dot(tri, lo, preferred_element_type=F32)) + carry_ref[...]
    o_ref[0] = cs
    carry_ref[...] = cs[tt - 1:tt, :]


def cumsum_time(x, tt=256):
    bsz, s, h = x.shape
    tt = min(tt, s)
    return pl.pallas_call(
        functools.partial(_cumsum_kernel, tt=tt),
        grid=(bsz, s // tt),
        in_specs=[pl.BlockSpec((1, tt, h), lambda b, i: (b, i, 0))],
        out_specs=pl.BlockSpec((1, tt, h), lambda b, i: (b, i, 0)),
        out_shape=jax.ShapeDtypeStruct((bsz, s, h), F32),
        scratch_shapes=[pltpu.VMEM((1, h), F32)],
        compiler_params=_params("parallel", "arbitrary"),
        name="cumsum_time",
    )(x)


def _logsig_kernel(z_ref, b_ref, o_ref):
    o_ref[...] = jax.nn.log_sigmoid(z_ref[...] + b_ref[...])


def log_sigmoid_bias(z, bias, tm=1024):
    m, n = z.shape
    tm = min(tm, m)
    return pl.pallas_call(
        _logsig_kernel,
        grid=(m // tm,),
        in_specs=[pl.BlockSpec((tm, n), lambda i: (i, 0)), pl.BlockSpec((1, n), lambda i: (0, 0))],
        out_specs=pl.BlockSpec((tm, n), lambda i: (i, 0)),
        out_shape=jax.ShapeDtypeStruct((m, n), F32),
        compiler_params=_params("parallel"),
        name="log_sigmoid_bias",
    )(z, bias.reshape(1, n).astype(F32))


_NTOP = PEER_TOPK + 1
_STAIR = [(a, _NTOP // (a + 1)) for a in range(_NTOP)]
_STAIR_ROWS = ((sum(nb for _, nb in _STAIR) + 7) // 8) * 8


def _peer_score_kernel(q_ref, keys_ref, s1_ref, e1_ref, thr_ref, e2_ref, top_ref, cand_ref, cv_ref):
    q = q_ref[...]
    top_ref[...] = jnp.full(top_ref.shape, -jnp.inf, F32)
    cv_ref[...] = jnp.full(cv_ref.shape, -jnp.inf, F32)
    sc = []
    for c in range(2):
        qc = q[:, c * PEER_NKEYS:(c + 1) * PEER_NKEYS].astype(BF16)
        kc = keys_ref[c].astype(BF16)
        s = lax.dot_general(kc, qc, (((1,), (1,)), ((), ())), preferred_element_type=F32)
        sc.append(s)
        for k in range(_NTOP):
            mx = jnp.max(s, axis=0, keepdims=True)
            top_ref[c, k:k + 1, :] = mx
            s = jnp.where(s == mx, -jnp.inf, s)
    v1 = top_ref[0]
    v2 = top_ref[1]
    cand_ref[...] = jnp.full(cand_ref.shape, -jnp.inf, F32)
    off = 0
    for a, nb in _STAIR:
        cand_ref[off:off + nb, :] = v1[a:a + 1, :] + v2[0:nb, :]
        off += nb
    cand = cand_ref[...]
    for k in range(_NTOP):
        mx = jnp.max(cand, axis=0, keepdims=True)
        cv_ref[k:k + 1, :] = mx
        cand = jnp.where(cand == mx, -jnp.inf, cand)
    cv = cv_ref[...]
    tau = 0.5 * (cv[PEER_TOPK - 1:PEER_TOPK, :] + cv[PEER_TOPK:PEER_TOPK + 1, :])
    z = jnp.sum(jnp.exp(cv[0:PEER_TOPK, :] - cv[0:1, :]), axis=0, keepdims=True)
    s1, s2 = sc
    s1_ref[0] = s1
    e1_ref[0] = jnp.exp(s1 - v1[0:1, :])
    thr_ref[0] = tau - s2
    e2_ref[0] = jnp.exp(s2 - v2[0:1, :]) / z


def peer_scores(q, keys, tm=256):
    t = q.shape[0]
    tm = min(tm, t)
    assert t % tm == 0 and tm % LANES == 0
    shp = jax.ShapeDtypeStruct((PEER_HEADS, PEER_NKEYS, t), F32)
    ospec = pl.BlockSpec((1, PEER_NKEYS, tm), lambda i, h: (h, 0, i))
    return pl.pallas_call(
        _peer_score_kernel,
        grid=(t // tm, PEER_HEADS),
        in_specs=[pl.BlockSpec((tm, 2 * PEER_NKEYS), lambda i, h: (i, h)),
                  pl.BlockSpec(keys.shape, lambda i, h: (0, 0, 0))],
        out_specs=[ospec, ospec, ospec, ospec],
        out_shape=[shp, shp, shp, shp],
        scratch_shapes=[pltpu.VMEM((2, _NTOP + 7, tm), F32), pltpu.VMEM((_STAIR_ROWS, tm), F32),
                        pltpu.VMEM((_NTOP + 7, tm), F32)],
        compiler_params=_params("parallel", "parallel"),
        name="peer_scores",
    )(q, keys)


GATE_ROWS = 16

def _peer_dense_kernel(x_ref, g_ref, u_ref, vt_ref, s1_ref, e1_ref, thr_ref, e2_ref, o_ref,
                       xn_ref, ht_ref, at_ref, acc_ref, srow_ref, *, ci, nc):
    s = pl.program_id(1)
    tm = x_ref.shape[0]

    @pl.when(s == 0)
    def _():
        x = x_ref[...]
        xn_ref[...] = (x * lax.rsqrt(jnp.mean(x * x, axis=-1, keepdims=True) + EPS) * g_ref[...]).astype(BF16)
        acc_ref[...] = jnp.zeros(acc_ref.shape, F32)
        ht_ref[...] = jnp.zeros(ht_ref.shape, F32)
        at_ref[...] = jnp.zeros(at_ref.shape, BF16)

    gate_live = jnp.logical_and(s >= 1, s <= nc)
    chunk = jnp.clip(s - 1, 0, nc - 1)
    slab = pl.ds(pl.multiple_of(chunk * ci, ci), ci)
    for h in range(PEER_HEADS):
        srow_ref[0, h] = s1_ref[h, slab, :]
        srow_ref[1, h] = e1_ref[h, slab, :]

    def stages(p):
        q = 1 - p
        ht_ref[p] = lax.dot_general(u_ref[...], xn_ref[...], (((1,), (1,)), ((), ())),
                                    preferred_element_type=F32)
        for t0 in range(0, tm, LANES):
            for j0 in range(0, PEER_NKEYS, GATE_ROWS):
                gates = [jnp.zeros((GATE_ROWS, LANES), F32) for _ in range(ci)]
                for h in range(PEER_HEADS):
                    thr = thr_ref[h, j0:j0 + GATE_ROWS, t0:t0 + LANES]
                    e2 = e2_ref[h, j0:j0 + GATE_ROWS, t0:t0 + LANES]
                    for i in range(ci):
                        s1 = srow_ref[0, h, i:i + 1, t0:t0 + LANES]
                        e1 = srow_ref[1, h, i:i + 1, t0:t0 + LANES]
                        gates[i] = gates[i] + jnp.where(s1 >= thr, e2, 0.0) * e1
                for i in range(ci):
                    r0 = i * PEER_NKEYS + j0
                    a = gates[i] * jax.nn.gelu(ht_ref[q, r0:r0 + GATE_ROWS, t0:t0 + LANES])
                    at_ref[q, r0:r0 + GATE_ROWS, t0:t0 + LANES] = jnp.where(gate_live, a, 0.0).astype(BF16)
        acc_ref[...] += jnp.dot(vt_ref[...], at_ref[p], preferred_element_type=F32)

    pl.when(s % 2 == 0)(functools.partial(stages, 0))
    pl.when(s % 2 == 1)(functools.partial(stages, 1))

    @pl.when(s == nc + 1)
    def _():
        o_ref[...] = x_ref[...] + acc_ref[...].T


def peer_dense(x, gain, u_all, vt_all, layer, s1, e1, thr, e2, tm=512, ci=8):
    t, d = x.shape
    ne = u_all.shape[1]
    tm = min(tm, t)
    ec = ci * PEER_NKEYS
    assert t % tm == 0 and ne % ec == 0 and ci == 8
    nc = ne // ec
    once = pl.Buffered(1)
    sspec = pl.BlockSpec((PEER_HEADS, PEER_NKEYS, tm), lambda i, s: (0, 0, i), pipeline_mode=once)
    return pl.pallas_call(
        functools.partial(_peer_dense_kernel, ci=ci, nc=nc),
        grid=(t // tm, nc + 2),
        in_specs=[pl.BlockSpec((tm, d), lambda i, s: (i, 0), pipeline_mode=once),
                  pl.BlockSpec((1, d), lambda i, s: (0, 0)),
                  pl.BlockSpec((None, ec, d), lambda i, s: (layer, jnp.minimum(s, nc - 1), 0)),
                  pl.BlockSpec((None, d, ec), lambda i, s: (layer, 0, jnp.maximum(s - 2, 0))),
                  sspec, sspec, sspec, sspec],
        out_specs=pl.BlockSpec((tm, d), lambda i, s: (i, 0)),
        out_shape=jax.ShapeDtypeStruct((t, d), F32),
        scratch_shapes=[pltpu.VMEM((tm, d), BF16), pltpu.VMEM((2, ec, tm), F32),
                        pltpu.VMEM((2, ec, tm), BF16), pltpu.VMEM((d, tm), F32),
                        pltpu.VMEM((2, PEER_HEADS, ci, tm), F32)],
        compiler_params=_params("parallel", "arbitrary"),
        name="peer_dense",
    )(x, gain.reshape(1, d).astype(F32), u_all, vt_all, s1, e1, thr, e2)


def peer_layer(x, gain, wq_bf, keys, u_all, vt_all, layer):
    t = x.shape[0]
    tp = -(-t // LANES) * LANES
    xp = jnp.pad(x, ((0, tp - t), (0, 0))) if tp != t else x
    q = rms_matmul(xp, gain, wq_bf)
    s1, e1, thr, e2 = peer_scores(q, keys)
    out = peer_dense(xp, gain, u_all, vt_all, layer, s1, e1, thr, e2)
    return out[:t]


def kernel(x_prompt, x_sample, mem_prompt, cache_diff_k, cache_diff_v, cache_fox_k, cache_fox_v, cache_fox_logf, state_conv, cache_mem_k, cache_mem_v, page_table, norm_mix, norm_xattn, norm_ffn, even_w_in, even_w_out, diff_q_gain, diff_k_gain, diff_lambda, diff_out_gain, conv_w, conv_b, conv_ln_g, conv_ln_b, odd_w_in, fox_forget_bias, fox_q_gain, fox_k_gain, odd_w_out, mem_norm, xattn_wq, xattn_wk, xattn_wv, xattn_wo, xattn_q_gain, xattn_k_gain, peer_wq, peer_keys, peer_u, peer_v):
    bp, seq, d = x_prompt.shape
    bs = x_sample.shape[0]
    depth = norm_mix.shape[0]
    half = d // 2
    n_pages = page_table.shape[1]
    past = n_pages * PAGE
    n_phys = cache_diff_k.shape[1]
    mem_len = mem_prompt.shape[1]
    xw = xattn_wq.shape[2]
    tp = bp * seq
    srow = 8

    xp = x_prompt.reshape(tp, d)
    xs = x_sample.reshape(bs, d)
    memf = mem_prompt.reshape(bp * mem_len, d)

    n_even, n_odd = cache_diff_k.shape[0], cache_fox_k.shape[0]
    dk_t = jnp.transpose(cache_diff_k, (0, 1, 3, 4, 5, 2)).reshape(n_even * n_phys, DEC_ROWS, DIFF_DK, PAGE)
    dv_c = cache_diff_v.reshape(n_even * n_phys, PAGE * DIFF_HEADS, LANES)
    fk_c = cache_fox_k.reshape(n_odd * n_phys, PAGE * FOX_HEADS, LANES)
    fv_c = cache_fox_v.reshape(n_odd * n_phys, PAGE * FOX_HEADS, LANES)
    lf_t = jnp.swapaxes(cache_fox_logf, 2, 3).reshape(n_odd * n_phys, FOX_HEADS, PAGE)

    u_all = peer_u.astype(BF16)
    vt_all = jnp.swapaxes(peer_v.astype(BF16), 1, 2)

    rope_p = rope_tables(jnp.arange(seq, dtype=I32), seq)
    rope_s = rope_tables(jnp.full((1,), past, I32), bs)

    outs = {k: [] for k in ("dk_p", "dv_p", "dk_s", "dv_s", "fk_p", "fv_p", "fl_p", "fk_s", "fv_s", "fl_s",
                            "cv_p", "cv_s", "mk_p", "mv_p")}

    for l in range(depth):
        i = l // 2
        if l % 2 == 0:
            lam_init = 0.8 - 0.6 * math.exp(-0.3 * l)
            w_in = even_w_in[i].astype(BF16)
            w_out = even_w_out[i].astype(BF16)
            nq = half // LANES
            res = []
            for x, rope in ((xp, rope_p), (xs, rope_s)):
                z = rms_matmul(x, norm_mix[l], w_in)
                qn = head_norm(z, 0, nq, diff_q_gain[i], DIFF_DK, rope)
                kn = head_norm(z, nq, nq, diff_k_gain[i], DIFF_DK, rope)
                v = z[:, 2 * half:3 * half]
                u = glu(z, 3 * half, 4 * half, half)
                res.append((z, qn, kn, v, u))
            z, qn, kn, v, u = res[0]
            attn = flash_attention(qn, 0, kn, 0, z, 2 * nq, bsz=bp, heads=DIFF_HEADS, sq=seq, sk=seq,
                                   mode="diff", causal=True, scale=DIFF_DK ** -0.5, tq=512, tk=512,
                                   lam_params=diff_lambda[i], lam_init=lam_init)
            attn = head_norm(attn, 0, nq, diff_out_gain[i], LANES, out_scale=1.0 - lam_init)
            u3 = u.reshape(bp, seq, half)
            conv = conv_branch(u3, jnp.zeros((bp, CONV_HALO, half), F32), conv_w[i], conv_b[i],
                               conv_ln_g[i], conv_ln_b[i]).reshape(tp, half)
            xp = rms_matmul(jnp.concatenate([attn, conv], axis=1), None, w_out, res=xp)
            outs["dk_p"].append(kn.reshape(bp, seq, DIFF_HEADS, 2, DIFF_DK))
            outs["dv_p"].append(v.reshape(bp, seq, DIFF_HEADS, 2 * DIFF_DK))
            outs["cv_p"].append(u3[:, seq - (CONV_WIDTH - 1):])
            z, qn, kn, v, u = res[1]
            attn = decode_attention(qn.reshape(bs, DEC_ROWS, DIFF_DK), kn.reshape(bs, DEC_ROWS, DIFF_DK),
                                    v.reshape(bs, DIFF_HEADS, LANES), dk_t, dv_c, page_table, i * n_phys,
                                    mode="diff", scale=DIFF_DK ** -0.5,
                                    lam_params=diff_lambda[i], lam_init=lam_init).reshape(bs, half)
            attn = head_norm(attn, 0, nq, diff_out_gain[i], LANES, out_scale=1.0 - lam_init)
            st = state_conv[i]
            init = jnp.concatenate([jnp.zeros((bs, CONV_HALO - (CONV_WIDTH - 1), half), F32), st], axis=1)
            u_pad = jnp.concatenate([u[:, None, :], jnp.zeros((bs, srow - 1, half), F32)], axis=1)
            conv = conv_branch(u_pad, init, conv_w[i], conv_b[i], conv_ln_g[i], conv_ln_b[i])[:, 0]
            xs = rms_matmul(jnp.concatenate([attn, conv], axis=1), None, w_out, res=xs)
            outs["dk_s"].append(kn.reshape(bs, 1, DIFF_HEADS, 2, DIFF_DK))
            outs["dv_s"].append(v.reshape(bs, 1, DIFF_HEADS, 2 * DIFF_DK))
            outs["cv_s"].append(jnp.concatenate([st[:, 1:], u[:, None, :]], axis=1))
        else:
            w_in = odd_w_in[i]
            w_main = w_in[:, :3 * d].astype(BF16)
            w_gate = jnp.pad(w_in[:, 3 * d:], ((0, 0), (0, LANES - FOX_HEADS))).astype(BF16)
            w_out = odd_w_out[i].astype(BF16)
            nh = d // LANES
            fbias = jnp.pad(fox_forget_bias[i], (0, LANES - FOX_HEADS))
            res = []
            for x in (xp, xs):
                z = rms_matmul(x, norm_mix[l], w_main)
                zg = rms_matmul(x, norm_mix[l], w_gate)
                qn = head_norm(z, 0, nh, fox_q_gain[i], LANES)
                kn = head_norm(z, nh, nh, fox_k_gain[i], LANES)
                v = z[:, 2 * d:3 * d]
                logf = log_sigmoid_bias(zg, fbias)[:, :FOX_HEADS]
                res.append((z, qn, kn, v, logf))
            z, qn, kn, v, logf = res[0]
            cum = cumsum_time(logf.reshape(bp, seq, FOX_HEADS))
            cum_t = jnp.swapaxes(cum, 1, 2).reshape(bp * FOX_HEADS, 1, seq)
            attn = flash_attention(qn, 0, kn, 0, z, 2 * nh, bsz=bp, heads=FOX_HEADS, sq=seq, sk=seq,
                                   mode="fox", causal=True, scale=LANES ** -0.5, tq=1024, tk=512, ck=cum_t)
            xp = rms_matmul(attn, None, w_out, res=xp)
            outs["fk_p"].append(kn.reshape(bp, seq, FOX_HEADS, LANES))
            outs["fv_p"].append(v.reshape(bp, seq, FOX_HEADS, LANES))
            outs["fl_p"].append(logf.reshape(bp, seq, FOX_HEADS))
            z, qn, kn, v, logf = res[1]
            h3 = (bs, FOX_HEADS, LANES)
            attn = decode_attention(qn.reshape(h3), kn.reshape(h3), v.reshape(h3), fk_c, fv_c, page_table,
                                    i * n_phys, mode="fox", scale=LANES ** -0.5,
                                    lf_cache_t=lf_t, lf_new=logf).reshape(bs, d)
            xs = rms_matmul(attn, None, w_out, res=xs)
            outs["fk_s"].append(kn.reshape(bs, 1, FOX_HEADS, LANES))
            outs["fv_s"].append(v.reshape(bs, 1, FOX_HEADS, LANES))
            outs["fl_s"].append(logf.reshape(bs, 1, FOX_HEADS))

        wq = xattn_wq[l].astype(BF16)
        wo = xattn_wo[l].astype(BF16)
        wkv = jnp.concatenate([xattn_wk[l], xattn_wv[l]], axis=1).astype(BF16)
        nxh = xw // LANES
        kv = rms_matmul(memf, mem_norm[l], wkv)
        mk = head_norm(kv, 0, nxh, xattn_k_gain[l], LANES)
        mv = kv[:, xw:]
        outs["mk_p"].append(mk.reshape(bp, mem_len, X_HEADS, LANES))
        outs["mv_p"].append(mv.reshape(bp, mem_len, X_HEADS, LANES))
        xq = head_norm(rms_matmul(xp, norm_xattn[l], wq), 0, nxh, xattn_q_gain[l], LANES)
        o = flash_attention(xq, 0, mk, 0, kv, nxh, bsz=bp, heads=X_HEADS, sq=seq, sk=mem_len,
                            mode="plain", causal=False, scale=LANES ** -0.5, tq=1024, tk=mem_len)
        xp = rms_matmul(o, None, wo, res=xp)
        xq = head_norm(rms_matmul(xs, norm_xattn[l], wq), 0, nxh, xattn_q_gain[l], LANES)
        xq = jnp.pad(xq[:, None, :], ((0, 0), (0, srow - 1), (0, 0))).reshape(bs * srow, xw)
        o = flash_attention(xq, 0, cache_mem_k[l].reshape(bs * mem_len, xw), 0,
                            cache_mem_v[l].reshape(bs * mem_len, xw), 0, bsz=bs, heads=X_HEADS,
                            sq=srow, sk=mem_len, mode="plain", causal=False, scale=LANES ** -0.5,
                            tq=srow, tk=mem_len)
        o = o.reshape(bs, srow, xw)[:, 0]
        xs = rms_matmul(o, None, wo, res=xs)

        pwq = peer_wq[l].astype(BF16)
        xp = peer_layer(xp, norm_ffn[l], pwq, peer_keys[l], u_all, vt_all, l)
        xs = peer_layer(xs, norm_ffn[l], pwq, peer_keys[l], u_all, vt_all, l)

    st = lambda k: jnp.stack(outs[k])
    return (xp.reshape(bp, seq, d), xs.reshape(bs, 1, d),
            st("dk_p"), st("dv_p"), st("fk_p"), st("fv_p"), st("fl_p"),
            st("cv_p"), st("mk_p"), st("mv_p"),
            st("dk_s"), st("dv_s"), st("fk_s"), st("fv_s"), st("fl_s"), st("cv_s"))
```
